```python
import jax, jax.numpy as jnp
from jax import lax
import numpy as np

D_MODEL = 1024
BATCH = 16
SEQ = 2048
DEPTH = 1

D_MIX = D_MODEL
HEAD_DIM = 64
ATT_WIDTH = D_MIX // 2
ATT_Q_HEADS = ATT_WIDTH // HEAD_DIM
ATT_KV_HEADS = 2
ATT_GROUP = ATT_Q_HEADS // ATT_KV_HEADS
ATT_KV_WIDTH = ATT_KV_HEADS * HEAD_DIM
Q_BLOCK = 128
ROPE_THETA = 10000.0
GRID_W = 64
RWKV_WIDTH = D_MIX - ATT_WIDTH
RWKV_HEADS = RWKV_WIDTH // HEAD_DIM
DECAY_RANK = 64
ICLR_RANK = 64
N_DIRS = 2
RWKV_SHIFT_WIDTH = 3 * RWKV_WIDTH + DECAY_RANK + ICLR_RANK
IN_SPLITS = (ATT_WIDTH, ATT_KV_WIDTH, ATT_KV_WIDTH, ATT_WIDTH, RWKV_SHIFT_WIDTH, RWKV_WIDTH)
IN_WIDTH = ATT_WIDTH + 2 * ATT_KV_WIDTH + ATT_WIDTH + RWKV_SHIFT_WIDTH + RWKV_WIDTH
DECAY_SCALE = 0.6065306597126334
NORM_EPS = 1e-6
GN_EPS = 64e-5
L2_EPS = 1e-12

kernel_name = "hybrid_gqa_rwkv7_parallel_heads"


def rms_norm(x, g, eps=NORM_EPS):
    xf = x.astype(jnp.float32)
    y = xf * lax.rsqrt(jnp.mean(xf * xf, axis=-1, keepdims=True) + eps)
    return (y * g.astype(jnp.float32)).astype(x.dtype)


def axial_rope_tables(T, dtype):
    rows = T // GRID_W
    row = jnp.repeat(jnp.arange(rows, dtype=jnp.float32), GRID_W)
    col = jnp.tile(jnp.arange(GRID_W, dtype=jnp.float32), rows)
    n_freq = HEAD_DIM // 4
    inv_freq = ROPE_THETA ** (-jnp.arange(n_freq, dtype=jnp.float32) / n_freq)
    ang_r = row[:, None, None] * inv_freq
    ang_c = col[:, None, None] * inv_freq
    return (jnp.cos(ang_r).astype(dtype), jnp.sin(ang_r).astype(dtype),
            jnp.cos(ang_c).astype(dtype), jnp.sin(ang_c).astype(dtype))


def rotate_axis(x, cos, sin):
    n = x.shape[-1] // 2
    x1, x2 = x[..., :n], x[..., n:]
    return jnp.concatenate([x1 * cos - x2 * sin, x2 * cos + x1 * sin], axis=-1)


def apply_axial_rope(x, tables):
    cos_r, sin_r, cos_c, sin_c = tables
    half = HEAD_DIM // 2
    return jnp.concatenate([rotate_axis(x[..., :half], cos_r, sin_r),
                            rotate_axis(x[..., half:], cos_c, sin_c)], axis=-1)


def axial_gqa_attention(q, k, v, q_norm_g, k_norm_g):
    B, T, _ = q.shape
    q = rms_norm(q.reshape(B, T, ATT_Q_HEADS, HEAD_DIM), q_norm_g)
    k = rms_norm(k.reshape(B, T, ATT_KV_HEADS, HEAD_DIM), k_norm_g)
    v = v.reshape(B, T, ATT_KV_HEADS, HEAD_DIM)
    tables = axial_rope_tables(T, q.dtype)
    q = apply_axial_rope(q, tables)
    k = apply_axial_rope(k, tables)
    n_blk = T // Q_BLOCK
    q_blocks = q.reshape(B, n_blk, Q_BLOCK, ATT_KV_HEADS, ATT_GROUP, HEAD_DIM).transpose(1, 0, 3, 4, 2, 5)
    k_t = k.transpose(0, 2, 1, 3)
    v_t = v.transpose(0, 2, 1, 3)
    scale = HEAD_DIM ** -0.5

    def attend(q_blk):
        s = jnp.einsum('bkgqd,bksd->bkgqs', q_blk, k_t).astype(jnp.float32) * scale
        p = jax.nn.softmax(s, axis=-1).astype(v_t.dtype)
        return jnp.einsum('bkgqs,bksd->bkgqd', p, v_t)

    o = lax.map(attend, q_blocks)
    return o.transpose(1, 0, 4, 2, 3, 5).reshape(B, T, ATT_WIDTH)


def centered_shift(x, taps):
    xp = jnp.pad(x, ((0, 0), (1, 1), (0, 0)))
    return taps[0] * xp[:, :-2] + taps[1] * xp[:, 1:-1] + taps[2] * xp[:, 2:]


def bidir_rwkv7(r, k, v, wd, ad, w_up, w0, a_up, a0, k_k, k_a, r_k, gn_w, gn_b):
    f32 = jnp.float32
    B, T, _ = r.shape
    H, N = RWKV_HEADS, HEAD_DIM
    rf, kf, vf = r.astype(f32), k.astype(f32), v.astype(f32)
    heads = lambda z: z.reshape(z.shape[:-1] + (H, N))
    w = jnp.exp(-DECAY_SCALE * jax.nn.sigmoid(
        w0.astype(f32)[:, None, None, :]
        + jnp.einsum('btr,zrc->zbtc', jnp.tanh(wd.astype(f32)), w_up.astype(f32))))
    a = jax.nn.sigmoid(a0.astype(f32)[:, None, None, :]
                       + jnp.einsum('btr,zrc->zbtc', ad.astype(f32), a_up.astype(f32)))
    kk = heads(kf * k_k.astype(f32))
    kk = kk * lax.rsqrt(jnp.sum(kk * kk, axis=-1, keepdims=True) + L2_EPS)
    kt = kf[None] * (1.0 + (a - 1.0) * k_a.astype(f32))
    akk = heads(a) * kk[None]

    both = lambda z: jnp.stack([z, z])

    def time_major(z):
        z = jnp.stack([z[0], jnp.flip(z[1], axis=1)])
        return jnp.moveaxis(z, 2, 0)

    xs = (time_major(heads(w)), time_major(heads(kt)), time_major(both(heads(vf))),
          time_major(both(heads(rf))), time_major(both(kk)), time_major(akk))
    S0 = jnp.zeros((N_DIRS, B, H, N, N), f32)

    def step(S, inp):
        w_t, k_t, v_t, r_t, kk_t, akk_t = inp
        S = (S * w_t[..., None, :]
             - jnp.einsum('zbhvk,zbhk->zbhv', S, kk_t)[..., None] * akk_t[..., None, :]
             + v_t[..., :, None] * k_t[..., None, :])
        return S, jnp.einsum('zbhvk,zbhk->zbhv', S, r_t)

    _, ys = lax.scan(step, S0, xs)
    ys = jnp.moveaxis(ys, 0, 2)
    y = ys[0] + jnp.flip(ys[1], axis=1)
    mu = jnp.mean(y, axis=-1, keepdims=True)
    var = jnp.mean((y - mu) ** 2, axis=-1, keepdims=True)
    y = (y - mu) * lax.rsqrt(var + GN_EPS) * heads(gn_w.astype(f32)) + heads(gn_b.astype(f32))
    bonus = jnp.einsum('bthn,zbthn,hn->bth', heads(rf), heads(kt), r_k.astype(f32))[..., None] * heads(vf)
    return (y + bonus).reshape(B, T, RWKV_WIDTH)


def setup_inputs(seed: int = 0) -> dict:
    key = jax.random.key(seed)
    ks = jax.random.split(key, 24)
    L, D = DEPTH, D_MODEL
    nrm = jax.random.normal
    x = nrm(ks[0], (BATCH, SEQ, D), jnp.float32)
    c = nrm(ks[1], (BATCH, D), jnp.float32)
    w_ada = nrm(ks[2], (L, D, 3 * D), jnp.float32) * D ** -0.5
    b_ada = 0.01 * nrm(ks[3], (L, 3 * D), jnp.float32)
    g_pre = 1.0 + 0.05 * nrm(ks[4], (L, D), jnp.float32)
    w_in = nrm(ks[5], (L, D, IN_WIDTH), jnp.float32) * D ** -0.5
    q_norm_g = 1.0 + 0.05 * nrm(ks[6], (L, HEAD_DIM), jnp.float32)
    k_norm_g = 1.0 + 0.05 * nrm(ks[7], (L, HEAD_DIM), jnp.float32)
    shift_taps = (jnp.array([0.25, 1.0, 0.25], jnp.float32)[None, :, None]
                  + 0.05 * nrm(ks[8], (L, 3, RWKV_SHIFT_WIDTH), jnp.float32))
    w_up = 0.1 * nrm(ks[9], (L, N_DIRS, DECAY_RANK, RWKV_WIDTH), jnp.float32)
    w0 = jax.random.uniform(ks[10], (L, N_DIRS, RWKV_WIDTH), jnp.float32, -3.0, 3.0)
    a_up = 0.1 * nrm(ks[11], (L, N_DIRS, ICLR_RANK, RWKV_WIDTH), jnp.float32)
    a0 = 0.5 * nrm(ks[12], (L, N_DIRS, RWKV_WIDTH), jnp.float32)
    k_k = 0.85 + 0.05 * nrm(ks[13], (L, RWKV_WIDTH), jnp.float32)
    k_a = 1.0 + 0.05 * nrm(ks[14], (L, RWKV_WIDTH), jnp.float32)
    r_k = 0.1 * nrm(ks[15], (L, RWKV_HEADS, HEAD_DIM), jnp.float32)
    gn_w = 1.0 + 0.05 * nrm(ks[16], (L, RWKV_WIDTH), jnp.float32)
    gn_b = 0.01 * nrm(ks[17], (L, RWKV_WIDTH), jnp.float32)
    w_out = nrm(ks[18], (L, D_MIX, D), jnp.float32) * D_MIX ** -0.5
    g_post = 1.0 + 0.05 * nrm(ks[19], (L, D), jnp.float32)
    return {"x": x, "c": c, "w_ada": w_ada, "b_ada": b_ada, "g_pre": g_pre, "w_in": w_in,
            "q_norm_g": q_norm_g, "k_norm_g": k_norm_g, "shift_taps": shift_taps,
            "w_up": w_up, "w0": w0, "a_up": a_up, "a0": a0, "k_k": k_k, "k_a": k_a,
            "r_k": r_k, "gn_w": gn_w, "gn_b": gn_b, "w_out": w_out, "g_post": g_post}


def reference(x, c, w_ada, b_ada, g_pre, w_in, q_norm_g, k_norm_g, shift_taps,
              w_up, w0, a_up, a0, k_k, k_a, r_k, gn_w, gn_b, w_out, g_post):
    split_idx = [int(i) for i in np.cumsum(IN_SPLITS)[:-1]]
    rwkv_idx = [RWKV_WIDTH, 2 * RWKV_WIDTH, 3 * RWKV_WIDTH, 3 * RWKV_WIDTH + DECAY_RANK]
    c_act = jax.nn.silu(c)
    for l in range(DEPTH):
        mod = jnp.einsum('bd,de->be', c_act, w_ada[l]) + b_ada[l]
        shift, scale, gate = jnp.split(mod, 3, axis=-1)
        h = rms_norm(x, g_pre[l]) * (1.0 + scale[:, None, :]) + shift[:, None, :]
        proj = jnp.einsum('btd,dp->btp', h, w_in[l])
        q, k, v, g_att, rwkv_in, g_rwkv = jnp.split(proj, split_idx, axis=-1)
        y_att = axial_gqa_attention(q, k, v, q_norm_g[l], k_norm_g[l])
        rwkv_in = centered_shift(rwkv_in, shift_taps[l])
        r, kr, vr, wd, ad = jnp.split(rwkv_in, rwkv_idx, axis=-1)
        y_rwkv = bidir_rwkv7(r, kr, vr, wd, ad, w_up[l], w0[l], a_up[l], a0[l],
                             k_k[l], k_a[l], r_k[l], gn_w[l], gn_b[l]).astype(x.dtype)
        mixed = jnp.concatenate([y_att * jax.nn.silu(g_att), y_rwkv * jax.nn.silu(g_rwkv)], axis=-1)
        out = jnp.einsum('btm,md->btd', mixed, w_out[l])
        x = x + gate[:, None, :] * rms_norm(out, g_post[l])
    return x
```

```python
import functools

import jax
import jax.numpy as jnp
from jax import lax
from jax.experimental import pallas as pl
from jax.experimental.pallas import tpu as pltpu

F32 = jnp.float32
BF16 = jnp.bfloat16

HEAD_DIM = 64
ATT_Q_HEADS = 8
ATT_KV_HEADS = 2
ATT_GROUP = ATT_Q_HEADS // ATT_KV_HEADS
ATT_WIDTH = ATT_Q_HEADS * HEAD_DIM
ATT_KV_WIDTH = ATT_KV_HEADS * HEAD_DIM
RWKV_HEADS = 8
RWKV_WIDTH = RWKV_HEADS * HEAD_DIM
LORA_RANK = 64
RWKV_SHIFT_WIDTH = 3 * RWKV_WIDTH + 2 * LORA_RANK
N_DIRS = 2
GRID_W = 64
ROPE_THETA = 10000.0
DECAY_SCALE = 0.6065306597126334
NORM_EPS = 1e-6
GN_EPS = 64e-5
L2_EPS = 1e-12

Q_OFF = 0
K_OFF = Q_OFF + ATT_WIDTH
V_OFF = K_OFF + ATT_KV_WIDTH
GATT_OFF = V_OFF + ATT_KV_WIDTH
RIN_OFF = GATT_OFF + ATT_WIDTH
GRW_OFF = RIN_OFF + RWKV_SHIFT_WIDTH
IN_WIDTH = GRW_OFF + RWKV_WIDTH

LANES = 128
SUBLANES = 8
CHUNK = 64
PAIR = 2 * HEAD_DIM
VMEM_LIMIT = 56 * 1024 * 1024


def _bdot(a, b):
    return jnp.dot(a.astype(BF16), b.astype(BF16), preferred_element_type=F32)


def _bdot_nt(a, b):
    return lax.dot_general(a.astype(BF16), b.astype(BF16), (((1,), (1,)), ((), ())),
                           preferred_element_type=F32)


def _bdot_tn(a, b):
    return lax.dot_general(a.astype(BF16), b.astype(BF16), (((0,), (0,)), ((), ())),
                           preferred_element_type=F32)


def _sigmoid(x):
    return 1.0 / (1.0 + jnp.exp(-x))


def _ada_kernel(c_ref, w_ref, b_ref, o_ref):
    c = c_ref[...]
    ca = c * _sigmoid(c)
    o_ref[...] = jnp.dot(ca, w_ref[...], precision=lax.Precision.HIGHEST,
                         preferred_element_type=F32) + b_ref[...]


def _ada_call(c, w_ada, b_ada):
    B, D = c.shape
    n_out = w_ada.shape[1]
    tn = D
    return pl.pallas_call(
        _ada_kernel,
        grid=(n_out // tn,),
        in_specs=[pl.BlockSpec((B, D), lambda j: (0, 0)),
                  pl.BlockSpec((D, tn), lambda j: (0, j)),
                  pl.BlockSpec((1, tn), lambda j: (0, j))],
        out_specs=pl.BlockSpec((B, tn), lambda j: (0, j)),
        out_shape=jax.ShapeDtypeStruct((B, n_out), F32),
        compiler_params=pltpu.CompilerParams(dimension_semantics=("arbitrary",),
                                             vmem_limit_bytes=VMEM_LIMIT),
    )(c, w_ada, b_ada.reshape(1, n_out))


def _rope(x, cos, sin_signed, first_half):
    n = x.shape[1]
    partner = jnp.where(first_half, pltpu.roll(x, n - 16, 1), pltpu.roll(x, 16, 1))
    return x * cos + partner * sin_signed


def _inproj_kernel(x_ref, mod_ref, gpre_ref, w_ref, qg_ref, kg_ref, cos_ref, sin_ref, ones_ref,
                   q_ref, k_ref, v_ref, gatt_ref, rin_ref, grw_ref):
    D = x_ref.shape[2]
    x = x_ref[0]
    ms = jnp.mean(x * x, axis=-1, keepdims=True)
    xn = x * lax.rsqrt(ms + NORM_EPS) * gpre_ref[...]
    mod = mod_ref[0]
    shift = mod[:, :D]
    scale = mod[:, D:2 * D]
    h = (xn * (1.0 + scale) + shift).astype(BF16)

    def proj(lo, width):
        return jnp.dot(h, w_ref[:, lo:lo + width], preferred_element_type=F32)

    cos = cos_ref[...]
    sin = sin_ref[...]
    lane = lax.broadcasted_iota(jnp.int32, cos.shape, 1)
    first_half = (lane % 32) < 16

    def head_norm(y, g):
        w = y.shape[1]
        msq = _bdot(y * y, ones_ref[:w, :w]) * (1.0 / HEAD_DIM)
        return y * lax.rsqrt(msq + NORM_EPS) * g

    q = head_norm(proj(Q_OFF, ATT_WIDTH), qg_ref[...])
    reps = ATT_WIDTH // LANES
    cos_q = jnp.concatenate([cos] * reps, axis=1)
    sin_q = jnp.concatenate([sin] * reps, axis=1)
    lane_q = lax.broadcasted_iota(jnp.int32, cos_q.shape, 1)
    fh_q = (lane_q % 32) < 16
    q = _rope(q, cos_q, sin_q, fh_q) * (HEAD_DIM ** -0.5)
    q_ref[0] = q.astype(BF16)

    k = head_norm(proj(K_OFF, ATT_KV_WIDTH), kg_ref[...])
    k_ref[0] = _rope(k, cos, sin, first_half).astype(BF16)

    v_ref[0] = proj(V_OFF, ATT_KV_WIDTH).astype(BF16)
    gatt_ref[0] = proj(GATT_OFF, ATT_WIDTH)
    rin_ref[0] = proj(RIN_OFF, RWKV_SHIFT_WIDTH)
    grw_ref[0] = proj(GRW_OFF, RWKV_WIDTH)


def _rope_tables(T):
    rows = T // GRID_W
    row = jnp.repeat(jnp.arange(rows, dtype=F32), GRID_W)
    col = jnp.tile(jnp.arange(GRID_W, dtype=F32), rows)
    n_freq = HEAD_DIM // 4
    inv_freq = ROPE_THETA ** (-jnp.arange(n_freq, dtype=F32) / n_freq)
    ang_r = row[:, None] * inv_freq
    ang_c = col[:, None] * inv_freq
    cr, sr, cc, sc = jnp.cos(ang_r), jnp.sin(ang_r), jnp.cos(ang_c), jnp.sin(ang_c)
    cos = jnp.concatenate([cr, cr, cc, cc], axis=1)
    sin = jnp.concatenate([-sr, sr, -sc, sc], axis=1)
    reps = LANES // HEAD_DIM
    return jnp.tile(cos, (1, reps)), jnp.tile(sin, (1, reps))


def _block_ones(n):
    i = jnp.arange(n) // HEAD_DIM
    return (i[:, None] == i[None, :]).astype(BF16)


def _inproj_call(x, mod3, g_pre, w_in_bf, qg, kg, cos, sin, ones, tm):
    B, T, D = x.shape
    grid = (T // tm, B)
    row = lambda t, b: (b, t, 0)
    const2 = lambda t, b: (0, 0)
    out_shapes = (
        jax.ShapeDtypeStruct((B, T, ATT_WIDTH), BF16),
        jax.ShapeDtypeStruct((B, T, ATT_KV_WIDTH), BF16),
        jax.ShapeDtypeStruct((B, T, ATT_KV_WIDTH), BF16),
        jax.ShapeDtypeStruct((B, T, ATT_WIDTH), F32),
        jax.ShapeDtypeStruct((B, T, RWKV_SHIFT_WIDTH), F32),
        jax.ShapeDtypeStruct((B, T, RWKV_WIDTH), F32),
    )
    out_specs = tuple(pl.BlockSpec((1, tm, s.shape[2]), row) for s in out_shapes)
    return pl.pallas_call(
        _inproj_kernel,
        grid=grid,
        in_specs=[pl.BlockSpec((1, tm, D), row),
                  pl.BlockSpec((1, 1, mod3.shape[2]), lambda t, b: (b, 0, 0)),
                  pl.BlockSpec((1, D), const2),
                  pl.BlockSpec((D, IN_WIDTH), const2),
                  pl.BlockSpec((1, ATT_WIDTH), const2),
                  pl.BlockSpec((1, ATT_KV_WIDTH), const2),
                  pl.BlockSpec((tm, LANES), lambda t, b: (t, 0)),
                  pl.BlockSpec((tm, LANES), lambda t, b: (t, 0)),
                  pl.BlockSpec((ATT_WIDTH, ATT_WIDTH), const2)],
        out_specs=out_specs,
        out_shape=out_shapes,
        compiler_params=pltpu.CompilerParams(dimension_semantics=("arbitrary", "arbitrary"),
                                             vmem_limit_bytes=VMEM_LIMIT),
    )(x, mod3, g_pre, w_in_bf, qg, kg, cos, sin, ones)


def _attn_kernel(q_ref, k_ref, v_ref, g_ref, o_ref):
    q = q_ref[0]
    k = k_ref[0]
    v = v_ref[0]
    outs = []
    for h in range(ATT_Q_HEADS):
        kv = h // ATT_GROUP
        qh = q[:, h * HEAD_DIM:(h + 1) * HEAD_DIM]
        kh = k[:, kv * HEAD_DIM:(kv + 1) * HEAD_DIM]
        vh = v[:, kv * HEAD_DIM:(kv + 1) * HEAD_DIM]
        s = lax.dot_general(qh, kh, (((1,), (1,)), ((), ())), preferred_element_type=F32)
        m = jnp.max(s, axis=-1, keepdims=True)
        p = jnp.exp(s - m)
        l = jnp.sum(p, axis=-1, keepdims=True)
        o = jnp.dot(p.astype(BF16), vh, preferred_element_type=F32)
        outs.append(o / l)
    y = jnp.concatenate(outs, axis=1)
    g = g_ref[0]
    o_ref[0] = (y * (g * _sigmoid(g))).astype(BF16)


def _attn_call(q, k, v, g_att, tq):
    B, T, _ = q.shape
    return pl.pallas_call(
        _attn_kernel,
        grid=(B, T // tq),
        in_specs=[pl.BlockSpec((1, tq, ATT_WIDTH), lambda b, i: (b, i, 0)),
                  pl.BlockSpec((1, T, ATT_KV_WIDTH), lambda b, i: (b, 0, 0)),
                  pl.BlockSpec((1, T, ATT_KV_WIDTH), lambda b, i: (b, 0, 0)),
                  pl.BlockSpec((1, tq, ATT_WIDTH), lambda b, i: (b, i, 0))],
        out_specs=pl.BlockSpec((1, tq, ATT_WIDTH), lambda b, i: (b, i, 0)),
        out_shape=jax.ShapeDtypeStruct((B, T, ATT_WIDTH), BF16),
        compiler_params=pltpu.CompilerParams(dimension_semantics=("arbitrary", "arbitrary"),
                                             vmem_limit_bytes=VMEM_LIMIT),
    )(q, k, v, g_att)


def _tri_inverse(a_strict, eye, blk8, blk16, blk32, blk64):
    a8 = jnp.where(blk8, a_strict, 0.0)
    a2 = _bdot(a8, a8)
    a4 = _bdot(a2, a2)
    t = _bdot(_bdot(eye - a8, eye + a2), eye + a4)
    for inner, outer in ((blk8, blk16), (blk16, blk32), (blk32, blk64)):
        a_off = jnp.where(jnp.logical_and(outer, jnp.logical_not(inner)), a_strict, 0.0)
        t = t - _bdot(_bdot(t, a_off), t)
    return t


def _rwkv_dir(z, sh, tri_ref, prm, masks, h_scr):
    (wup_ref, w0_ref, aup_ref, a0_ref, kk_ref, ka_ref, ones_ref) = prm
    (lane_lo, strict, incl, eye, blk8, blk16, blk32, blk64) = masks
    L = sh.shape[0]
    W = RWKV_WIDTH
    r = sh[:, 0:W]
    k = sh[:, W:2 * W]
    v = sh[:, 2 * W:3 * W]
    wd = sh[:, 3 * W:3 * W + LORA_RANK]
    ad = sh[:, 3 * W + LORA_RANK:3 * W + 2 * LORA_RANK]

    lw = -DECAY_SCALE * _sigmoid(w0_ref[z] + _bdot(jnp.tanh(wd), wup_ref[z]))
    a = _sigmoid(a0_ref[z] + _bdot(ad, aup_ref[z]))
    kkr = k * kk_ref[...]
    kk = kkr * lax.rsqrt(_bdot(kkr * kkr, ones_ref[...]) + L2_EPS)
    kt = k * (1.0 + (a - 1.0) * ka_ref[...])
    akk = a * kk

    lw_hi = lw.astype(BF16)
    lw_lo = (lw - lw_hi.astype(F32)).astype(BF16)
    tri = tri_ref[z]
    g = (jnp.dot(tri, lw_hi, preferred_element_type=F32)
         + jnp.dot(tri, lw_lo, preferred_element_type=F32))
    g_tot = jnp.sum(lw, axis=0, keepdims=True)
    e_in = jnp.exp(g)
    e_out = jnp.exp(-g)
    e_rem = jnp.exp(g_tot - g)
    kkg = kk * jnp.exp(g - lw)
    rg = r * e_in
    kd = kt * e_out
    bd = akk * e_out
    kdg = kt * e_rem
    bdg = akk * e_rem
    gam = jnp.exp(g_tot)

    def stack(x):
        return jnp.concatenate([jnp.where(lane_lo, x, 0.0), jnp.where(lane_lo, 0.0, x)], axis=0)

    ys = []
    for p in range(W // PAIR):
        sl = slice(p * PAIR, (p + 1) * PAIR)
        kkg_s, rg_s, kd_s, bd_s = stack(kkg[:, sl]), stack(rg[:, sl]), stack(kd[:, sl]), stack(bd[:, sl])
        kdg_s, bdg_s, v_s = stack(kdg[:, sl]), stack(bdg[:, sl]), stack(v[:, sl])
        n = 2 * L
        gram = _bdot_nt(jnp.concatenate([kkg_s, rg_s], axis=0), jnp.concatenate([kd_s, bd_s], axis=0))
        akk_m = jnp.where(strict, gram[:n, :n], 0.0)
        akb_m = jnp.where(strict, gram[:n, n:], 0.0)
        ark_m = jnp.where(incl, gram[n:, :n], 0.0)
        arb_m = jnp.where(incl, gram[n:, n:], 0.0)
        t_inv = _tri_inverse(akb_m, eye, blk8, blk16, blk32, blk64)
        av = _bdot(jnp.concatenate([akk_m, ark_m], axis=0), v_s)
        wu = _bdot(t_inv, jnp.concatenate([kkg_s, av[:n]], axis=1))
        aw = _bdot(arb_m, wu)
        qe = rg_s - aw[:, :PAIR]
        yv = av[n:] - aw[:, PAIR:]
        uv = wu[:, PAIR:]
        c0 = _bdot_tn(jnp.concatenate([kdg_s, bdg_s], axis=0), jnp.concatenate([v_s, -uv], axis=0))
        bw = _bdot_tn(bdg_s, wu[:, :PAIR])
        m = jnp.where(eye, jnp.broadcast_to(gam[:, sl], (n, PAIR)), 0.0) - bw
        idx = z * (W // PAIR) + p
        h0 = h_scr[idx]
        y_s = _bdot(qe, h0) + yv
        h_scr[idx] = _bdot(m, h0) + c0
        ys.append(y_s[:L] + y_s[L:])
    return jnp.concatenate(ys, axis=1), (r, kt, v)


def _rwkv_kernel(fm_ref, fp_ref, fn_ref, bm_ref, bp_ref, bn_ref, taps_ref, tri_ref,
                 wup_ref, w0_ref, aup_ref, a0_ref, kk_ref, ka_ref, rk_ref, ones_ref,
                 yf_ref, yb_ref, bon_ref, h_scr):
    c = pl.program_id(1)
    n_chunks = pl.num_programs(1)
    L = fm_ref.shape[1]

    @pl.when(c == 0)
    def _():
        h_scr[...] = jnp.zeros_like(h_scr)

    taps = taps_ref[...]
    n = 2 * L
    ri = lax.broadcasted_iota(jnp.int32, (n, n), 0)
    ci = lax.broadcasted_iota(jnp.int32, (n, n), 1)
    eye = ri == ci
    blk = lambda s: (ri // s) == (ci // s)
    blk8, blk16, blk32, blk64 = blk(8), blk(16), blk(32), blk(64)
    lane_lo = lax.broadcasted_iota(jnp.int32, (L, PAIR), 1) < HEAD_DIM
    row_id = lax.broadcasted_iota(jnp.int32, (L, 1), 0)
    prm = (wup_ref, w0_ref, aup_ref, a0_ref, kk_ref, ka_ref, ones_ref)

    def shifted(m_ref, p_ref, n_ref, chunk):
        main = m_ref[0]
        prev_row = jnp.where(chunk > 0, p_ref[0][SUBLANES - 1:SUBLANES], 0.0)
        next_row = jnp.where(chunk < n_chunks - 1, n_ref[0][0:1], 0.0)
        up = jnp.where(row_id == 0, prev_row, pltpu.roll(main, 1, 0))
        dn = jnp.where(row_id == L - 1, next_row, pltpu.roll(main, L - 1, 0))
        return taps[0:1] * up + taps[1:2] * main + taps[2:3] * dn

    sh_f = shifted(fm_ref, fp_ref, fn_ref, c)
    earlier_f = jnp.logical_and(blk64, ci < ri)
    masks_f = (lane_lo, earlier_f, jnp.logical_or(earlier_f, eye), eye, blk8, blk16, blk32, blk64)
    y_f, _ = _rwkv_dir(0, sh_f, tri_ref, prm, masks_f, h_scr)
    yf_ref[0] = y_f

    cb = n_chunks - 1 - c
    sh_b = shifted(bm_ref, bp_ref, bn_ref, cb)
    earlier_b = jnp.logical_and(blk64, ci > ri)
    masks_b = (lane_lo, earlier_b, jnp.logical_or(earlier_b, eye), eye, blk8, blk16, blk32, blk64)
    y_b, (r_b, kt1_b, v_b) = _rwkv_dir(1, sh_b, tri_ref, prm, masks_b, h_scr)
    yb_ref[0] = y_b

    W = RWKV_WIDTH
    ad_b = sh_b[:, 3 * W + LORA_RANK:3 * W + 2 * LORA_RANK]
    a0_b = _sigmoid(a0_ref[0] + _bdot(ad_b, aup_ref[0]))
    kt0_b = sh_b[:, W:2 * W] * (1.0 + (a0_b - 1.0) * ka_ref[...])
    rk_sum = r_b * (kt0_b + kt1_b) * rk_ref[...]
    rk_hi = rk_sum.astype(BF16)
    rk_lo = (rk_sum - rk_hi.astype(F32)).astype(BF16)
    ones = ones_ref[...]
    dots = (jnp.dot(rk_hi, ones, preferred_element_type=F32)
            + jnp.dot(rk_lo, ones, preferred_element_type=F32))
    bon_ref[0] = dots * v_b


def _rwkv_call(rin, taps, tri, w_up, w0, a_up, a0, k_k, k_a, r_k, ones):
    B, T, C = rin.shape
    L = CHUNK
    nC = T // L
    hb = L // SUBLANES
    n_hb = T // SUBLANES
    W = RWKV_WIDTH

    def main_f(b, c): return (b, c, 0)
    def prev_f(b, c): return (b, jnp.maximum(c * hb - 1, 0), 0)
    def next_f(b, c): return (b, jnp.minimum((c + 1) * hb, n_hb - 1), 0)
    def main_b(b, c): return (b, nC - 1 - c, 0)
    def prev_b(b, c): return (b, jnp.maximum((nC - 1 - c) * hb - 1, 0), 0)
    def next_b(b, c): return (b, jnp.minimum((nC - c) * hb, n_hb - 1), 0)

    const2 = lambda b, c: (0, 0)
    const3 = lambda b, c: (0, 0, 0)
    out_shape = tuple(jax.ShapeDtypeStruct((B, T, W), F32) for _ in range(3))
    return pl.pallas_call(
        _rwkv_kernel,
        grid=(B, nC),
        in_specs=[pl.BlockSpec((1, L, C), main_f),
                  pl.BlockSpec((1, SUBLANES, C), prev_f),
                  pl.BlockSpec((1, SUBLANES, C), next_f),
                  pl.BlockSpec((1, L, C), main_b),
                  pl.BlockSpec((1, SUBLANES, C), prev_b),
                  pl.BlockSpec((1, SUBLANES, C), next_b),
                  pl.BlockSpec((3, C), const2),
                  pl.BlockSpec((N_DIRS, L, L), const3),
                  pl.BlockSpec((N_DIRS, LORA_RANK, W), const3),
                  pl.BlockSpec((N_DIRS, 1, W), const3),
                  pl.BlockSpec((N_DIRS, LORA_RANK, W), const3),
                  pl.BlockSpec((N_DIRS, 1, W), const3),
                  pl.BlockSpec((1, W), const2),
                  pl.BlockSpec((1, W), const2),
                  pl.BlockSpec((1, W), const2),
                  pl.BlockSpec((W, W), const2)],
        out_specs=(pl.BlockSpec((1, L, W), main_f),
                   pl.BlockSpec((1, L, W), main_b),
                   pl.BlockSpec((1, L, W), main_b)),
        out_shape=out_shape,
        scratch_shapes=[pltpu.VMEM((N_DIRS * (W // PAIR), 2 * L, PAIR), F32)],
        compiler_params=pltpu.CompilerParams(dimension_semantics=("arbitrary", "arbitrary"),
                                             vmem_limit_bytes=VMEM_LIMIT),
    )(rin, rin, rin, rin, rin, rin, taps, tri, w_up, w0, a_up, a0, k_k, k_a, r_k, ones)


def _outproj_kernel(att_ref, yf_ref, yb_ref, bon_ref, grw_ref, x_ref, mod_ref, w_ref,
                    gnw_ref, gnb_ref, gpost_ref, ones_ref, o_ref):
    y = yf_ref[0] + yb_ref[0]
    ones = ones_ref[...]
    inv_n = 1.0 / HEAD_DIM
    mu = _bdot(y, ones) * inv_n
    d = y - mu
    var = _bdot(d * d, ones) * inv_n
    yn = d * lax.rsqrt(var + GN_EPS) * gnw_ref[...] + gnb_ref[...]
    g = grw_ref[0]
    rw = ((yn + bon_ref[0]) * (g * _sigmoid(g))).astype(BF16)
    out = (jnp.dot(att_ref[0], w_ref[:ATT_WIDTH, :], preferred_element_type=F32)
           + jnp.dot(rw, w_ref[ATT_WIDTH:, :], preferred_element_type=F32))
    ms = jnp.mean(out * out, axis=-1, keepdims=True)
    on = out * lax.rsqrt(ms + NORM_EPS) * gpost_ref[...]
    o_ref[0] = x_ref[0] + mod_ref[0] * on


def _outproj_call(att, yf, yb, bon, grw, x, mod3, w_out_bf, gn_w, gn_b, g_post, ones, tm):
    B, T, D = x.shape
    W = RWKV_WIDTH
    row = lambda t, b: (b, t, 0)
    const2 = lambda t, b: (0, 0)
    return pl.pallas_call(
        _outproj_kernel,
        grid=(T // tm, B),
        in_specs=[pl.BlockSpec((1, tm, ATT_WIDTH), row),
                  pl.BlockSpec((1, tm, W), row),
                  pl.BlockSpec((1, tm, W), row),
                  pl.BlockSpec((1, tm, W), row),
                  pl.BlockSpec((1, tm, W), row),
                  pl.BlockSpec((1, tm, D), row),
                  pl.BlockSpec((1, 1, D), lambda t, b: (b, 0, 2)),
                  pl.BlockSpec((ATT_WIDTH + W, D), const2),
                  pl.BlockSpec((1, W), const2),
                  pl.BlockSpec((1, W), const2),
                  pl.BlockSpec((1, D), const2),
                  pl.BlockSpec((W, W), const2)],
        out_specs=pl.BlockSpec((1, tm, D), row),
        out_shape=jax.ShapeDtypeStruct((B, T, D), F32),
        compiler_params=pltpu.CompilerParams(dimension_semantics=("arbitrary", "arbitrary"),
                                             vmem_limit_bytes=VMEM_LIMIT),
    )(att, yf, yb, bon, grw, x, mod3, w_out_bf, gn_w, gn_b, g_post, ones)


def _pick_tile(T, target):
    t = min(T, target)
    while T % t:
        t //= 2
    return t


def kernel(x, c, w_ada, b_ada, g_pre, w_in, q_norm_g, k_norm_g, shift_taps, w_up, w0, a_up, a0,
           k_k, k_a, r_k, gn_w, gn_b, w_out, g_post):
    B, T, D = x.shape
    depth = w_ada.shape[0]
    assert T % CHUNK == 0 and T % GRID_W == 0
    tm = _pick_tile(T, 256)
    tq = _pick_tile(T, 256)
    cos, sin = _rope_tables(T)
    ones = _block_ones(RWKV_WIDTH)
    ti = jnp.arange(CHUNK)
    tri = jnp.stack([ti[None, :] <= ti[:, None], ti[None, :] >= ti[:, None]]).astype(BF16)
    for l in range(depth):
        mod = _ada_call(c, w_ada[l], b_ada[l])
        mod3 = mod.reshape(B, 1, 3 * D)
        qg = jnp.tile(q_norm_g[l], ATT_Q_HEADS).reshape(1, ATT_WIDTH)
        kg = jnp.tile(k_norm_g[l], ATT_KV_HEADS).reshape(1, ATT_KV_WIDTH)
        q, k, v, g_att, rin, g_rw = _inproj_call(
            x, mod3, g_pre[l].reshape(1, D), w_in[l].astype(BF16), qg, kg, cos, sin, ones, tm)
        att = _attn_call(q, k, v, g_att, tq)
        yf, yb, bon = _rwkv_call(
            rin, shift_taps[l], tri, w_up[l].astype(BF16), w0[l].reshape(N_DIRS, 1, RWKV_WIDTH),
            a_up[l].astype(BF16), a0[l].reshape(N_DIRS, 1, RWKV_WIDTH),
            k_k[l].reshape(1, RWKV_WIDTH), k_a[l].reshape(1, RWKV_WIDTH),
            r_k[l].reshape(1, RWKV_WIDTH), ones)
        x = _outproj_call(att, yf, yb, bon, g_rw, x, mod3, w_out[l].astype(BF16),
                          gn_w[l].reshape(1, RWKV_WIDTH), gn_b[l].reshape(1, RWKV_WIDTH),
                          g_post[l].reshape(1, D), ones, tm)
    return x
```

```python
import functools

import jax
import jax.numpy as jnp
from jax import lax
from jax.experimental import pallas as pl
from jax.experimental.pallas import tpu as pltpu

F32 = jnp.float32
BF16 = jnp.bfloat16

HEAD_DIM = 64
ATT_Q_HEADS = 8
ATT_KV_HEADS = 2
ATT_GROUP = ATT_Q_HEADS // ATT_KV_HEADS
ATT_WIDTH = ATT_Q_HEADS * HEAD_DIM
ATT_KV_WIDTH = ATT_KV_HEADS * HEAD_DIM
RWKV_HEADS = 8
RWKV_WIDTH = RWKV_HEADS * HEAD_DIM
LORA_RANK = 64
RWKV_SHIFT_WIDTH = 3 * RWKV_WIDTH + 2 * LORA_RANK
N_DIRS = 2
GRID_W = 64
ROPE_THETA = 10000.0
DECAY_SCALE = 0.6065306597126334
NORM_EPS = 1e-6
GN_EPS = 64e-5
L2_EPS = 1e-12

Q_OFF = 0
K_OFF = Q_OFF + ATT_WIDTH
V_OFF = K_OFF + ATT_KV_WIDTH
GATT_OFF = V_OFF + ATT_KV_WIDTH
RIN_OFF = GATT_OFF + ATT_WIDTH
GRW_OFF = RIN_OFF + RWKV_SHIFT_WIDTH
IN_WIDTH = GRW_OFF + RWKV_WIDTH

LANES = 128
SUBLANES = 8
CHUNK = 64
PAIR = 2 * HEAD_DIM
VMEM_LIMIT = 56 * 1024 * 1024


def _bdot(a, b):
    return jnp.dot(a.astype(BF16), b.astype(BF16), preferred_element_type=F32)


def _bdot_nt(a, b):
    return lax.dot_general(a.astype(BF16), b.astype(BF16), (((1,), (1,)), ((), ())),
                           preferred_element_type=F32)


def _bdot_tn(a, b):
    return lax.dot_general(a.astype(BF16), b.astype(BF16), (((0,), (0,)), ((), ())),
                           preferred_element_type=F32)


def _sigmoid(x):
    return 1.0 / (1.0 + jnp.exp(-x))


def _ada_kernel(c_ref, w_ref, b_ref, o_ref):
    c = c_ref[...]
    ca = c * _sigmoid(c)
    o_ref[...] = jnp.dot(ca, w_ref[...], precision=lax.Precision.HIGHEST,
                         preferred_element_type=F32) + b_ref[...]


def _ada_call(c, w_ada, b_ada):
    B, D = c.shape
    n_out = w_ada.shape[1]
    tn = D
    return pl.pallas_call(
        _ada_kernel,
        name="ada_mod",
        grid=(n_out // tn,),
        in_specs=[pl.BlockSpec((B, D), lambda j: (0, 0)),
                  pl.BlockSpec((D, tn), lambda j: (0, j)),
                  pl.BlockSpec((1, tn), lambda j: (0, j))],
        out_specs=pl.BlockSpec((B, tn), lambda j: (0, j)),
        out_shape=jax.ShapeDtypeStruct((B, n_out), F32),
        compiler_params=pltpu.CompilerParams(dimension_semantics=("arbitrary",),
                                             vmem_limit_bytes=VMEM_LIMIT),
    )(c, w_ada, b_ada.reshape(1, n_out))


def _rope(x, cos, sin_signed, first_half):
    n = x.shape[1]
    partner = jnp.where(first_half, pltpu.roll(x, n - 16, 1), pltpu.roll(x, 16, 1))
    return x * cos + partner * sin_signed


def _inproj_kernel(x_ref, mod_ref, gpre_ref, w_ref, qg_ref, kg_ref, cos_ref, sin_ref, ones_ref,
                   q_ref, k_ref, v_ref, gatt_ref, rin_ref, grw_ref):
    D = x_ref.shape[2]
    x = x_ref[0]
    ms = jnp.mean(x * x, axis=-1, keepdims=True)
    xn = x * lax.rsqrt(ms + NORM_EPS) * gpre_ref[...]
    mod = mod_ref[0]
    shift = mod[:, :D]
    scale = mod[:, D:2 * D]
    h = (xn * (1.0 + scale) + shift).astype(BF16)

    def proj(lo, width):
        return jnp.dot(h, w_ref[:, lo:lo + width], preferred_element_type=F32)

    cos = cos_ref[...]
    sin = sin_ref[...]
    lane = lax.broadcasted_iota(jnp.int32, cos.shape, 1)
    first_half = (lane % 32) < 16

    def head_norm(y, g):
        w = y.shape[1]
        msq = _bdot(y * y, ones_ref[:w, :w]) * (1.0 / HEAD_DIM)
        return y * lax.rsqrt(msq + NORM_EPS) * g

    q = head_norm(proj(Q_OFF, ATT_WIDTH), qg_ref[...])
    reps = ATT_WIDTH // LANES
    cos_q = jnp.concatenate([cos] * reps, axis=1)
    sin_q = jnp.concatenate([sin] * reps, axis=1)
    lane_q = lax.broadcasted_iota(jnp.int32, cos_q.shape, 1)
    fh_q = (lane_q % 32) < 16
    q = _rope(q, cos_q, sin_q, fh_q) * (HEAD_DIM ** -0.5)
    q_ref[0] = q.astype(BF16)

    k = head_norm(proj(K_OFF, ATT_KV_WIDTH), kg_ref[...])
    k_ref[0] = _rope(k, cos, sin, first_half).astype(BF16)

    v_ref[0] = proj(V_OFF, ATT_KV_WIDTH).astype(BF16)
    gatt_ref[0] = proj(GATT_OFF, ATT_WIDTH)
    rin_ref[0] = proj(RIN_OFF, RWKV_SHIFT_WIDTH)
    grw_ref[0] = proj(GRW_OFF, RWKV_WIDTH)


def _rope_tables(T):
    rows = T // GRID_W
    row = jnp.repeat(jnp.arange(rows, dtype=F32), GRID_W)
    col = jnp.tile(jnp.arange(GRID_W, dtype=F32), rows)
    n_freq = HEAD_DIM // 4
    inv_freq = ROPE_THETA ** (-jnp.arange(n_freq, dtype=F32) / n_freq)
    ang_r = row[:, None] * inv_freq
    ang_c = col[:, None] * inv_freq
    cr, sr, cc, sc = jnp.cos(ang_r), jnp.sin(ang_r), jnp.cos(ang_c), jnp.sin(ang_c)
    cos = jnp.concatenate([cr, cr, cc, cc], axis=1)
    sin = jnp.concatenate([-sr, sr, -sc, sc], axis=1)
    reps = LANES // HEAD_DIM
    return jnp.tile(cos, (1, reps)), jnp.tile(sin, (1, reps))


def _block_ones(n):
    i = jnp.arange(n) // HEAD_DIM
    return (i[:, None] == i[None, :]).astype(BF16)


def _inproj_call(x, mod3, g_pre, w_in_bf, qg, kg, cos, sin, ones, tm):
    B, T, D = x.shape
    grid = (T // tm, B)
    row = lambda t, b: (b, t, 0)
    const2 = lambda t, b: (0, 0)
    out_shapes = (
        jax.ShapeDtypeStruct((B, T, ATT_WIDTH), BF16),
        jax.ShapeDtypeStruct((B, T, ATT_KV_WIDTH), BF16),
        jax.ShapeDtypeStruct((B, T, ATT_KV_WIDTH), BF16),
        jax.ShapeDtypeStruct((B, T, ATT_WIDTH), F32),
        jax.ShapeDtypeStruct((B, T, RWKV_SHIFT_WIDTH), F32),
        jax.ShapeDtypeStruct((B, T, RWKV_WIDTH), F32),
    )
    out_specs = tuple(pl.BlockSpec((1, tm, s.shape[2]), row) for s in out_shapes)
    return pl.pallas_call(
        _inproj_kernel,
        name="in_proj",
        grid=grid,
        in_specs=[pl.BlockSpec((1, tm, D), row),
                  pl.BlockSpec((1, 1, mod3.shape[2]), lambda t, b: (b, 0, 0)),
                  pl.BlockSpec((1, D), const2),
                  pl.BlockSpec((D, IN_WIDTH), const2),
                  pl.BlockSpec((1, ATT_WIDTH), const2),
                  pl.BlockSpec((1, ATT_KV_WIDTH), const2),
                  pl.BlockSpec((tm, LANES), lambda t, b: (t, 0)),
                  pl.BlockSpec((tm, LANES), lambda t, b: (t, 0)),
                  pl.BlockSpec((ATT_WIDTH, ATT_WIDTH), const2)],
        out_specs=out_specs,
        out_shape=out_shapes,
        compiler_params=pltpu.CompilerParams(dimension_semantics=("arbitrary", "arbitrary"),
                                             vmem_limit_bytes=VMEM_LIMIT),
    )(x, mod3, g_pre, w_in_bf, qg, kg, cos, sin, ones)


def _attn_kernel(q_ref, k_ref, v_ref, g_ref, o_ref):
    q = q_ref[0]
    k = k_ref[0]
    v = v_ref[0]
    outs = []
    for h in range(ATT_Q_HEADS):
        kv = h // ATT_GROUP
        qh = q[:, h * HEAD_DIM:(h + 1) * HEAD_DIM]
        kh = k[:, kv * HEAD_DIM:(kv + 1) * HEAD_DIM]
        vh = v[:, kv * HEAD_DIM:(kv + 1) * HEAD_DIM]
        s = lax.dot_general(qh, kh, (((1,), (1,)), ((), ())), preferred_element_type=F32)
        m = jnp.max(s, axis=-1, keepdims=True)
        p = jnp.exp(s - m)
        l = jnp.sum(p, axis=-1, keepdims=True)
        o = jnp.dot(p.astype(BF16), vh, preferred_element_type=F32)
        outs.append(o / l)
    y = jnp.concatenate(outs, axis=1)
    g = g_ref[0]
    o_ref[0] = (y * (g * _sigmoid(g))).astype(BF16)


def _attn_call(q, k, v, g_att, tq):
    B, T, _ = q.shape
    return pl.pallas_call(
        _attn_kernel,
        name="gqa_attn",
        grid=(B, T // tq),
        in_specs=[pl.BlockSpec((1, tq, ATT_WIDTH), lambda b, i: (b, i, 0)),
                  pl.BlockSpec((1, T, ATT_KV_WIDTH), lambda b, i: (b, 0, 0)),
                  pl.BlockSpec((1, T, ATT_KV_WIDTH), lambda b, i: (b, 0, 0)),
                  pl.BlockSpec((1, tq, ATT_WIDTH), lambda b, i: (b, i, 0))],
        out_specs=pl.BlockSpec((1, tq, ATT_WIDTH), lambda b, i: (b, i, 0)),
        out_shape=jax.ShapeDtypeStruct((B, T, ATT_WIDTH), BF16),
        compiler_params=pltpu.CompilerParams(dimension_semantics=("arbitrary", "arbitrary"),
                                             vmem_limit_bytes=VMEM_LIMIT),
    )(q, k, v, g_att)


def _tri_inverse_all(a_list, eye, blks):
    blk8, blk16, blk32, blk64 = blks
    a8 = [jnp.where(blk8, a, 0.0) for a in a_list]
    a2 = [_bdot(x, x) for x in a8]
    a4 = [_bdot(x, x) for x in a2]
    t = [_bdot(eye - x, eye + y) for x, y in zip(a8, a2)]
    t = [_bdot(x, eye + y) for x, y in zip(t, a4)]
    for inner, outer in ((blk8, blk16), (blk16, blk32), (blk32, blk64)):
        off = jnp.logical_and(outer, jnp.logical_not(inner))
        xs = [_bdot(x, jnp.where(off, a, 0.0)) for x, a in zip(t, a_list)]
        t = [x - _bdot(y, x) for x, y in zip(t, xs)]
    return t


def _rwkv_prep(z, sh, tri_ref, prm):
    (wup_ref, w0_ref, aup_ref, a0_ref, kk_ref, ka_ref, ones_ref) = prm
    W = RWKV_WIDTH
    r = sh[:, 0:W]
    k = sh[:, W:2 * W]
    v = sh[:, 2 * W:3 * W]
    wd = sh[:, 3 * W:3 * W + LORA_RANK]
    ad = sh[:, 3 * W + LORA_RANK:3 * W + 2 * LORA_RANK]

    lw = -DECAY_SCALE * _sigmoid(w0_ref[z] + _bdot(jnp.tanh(wd), wup_ref[z]))
    a = _sigmoid(a0_ref[z] + _bdot(ad, aup_ref[z]))
    kkr = k * kk_ref[...]
    kk = kkr * lax.rsqrt(_bdot(kkr * kkr, ones_ref[...]) + L2_EPS)
    kt = k * (1.0 + (a - 1.0) * ka_ref[...])
    akk = a * kk

    lw_hi = lw.astype(BF16)
    lw_lo = (lw - lw_hi.astype(F32)).astype(BF16)
    tri = tri_ref[z]
    g = (jnp.dot(tri, lw_hi, preferred_element_type=F32)
         + jnp.dot(tri, lw_lo, preferred_element_type=F32))
    g_tot = jnp.sum(lw, axis=0, keepdims=True)
    e_out = jnp.exp(-g)
    e_rem = jnp.exp(g_tot - g)
    return dict(r=r, kt=kt, v=v,
                kkg=kk * jnp.exp(g - lw), rg=r * jnp.exp(g), kd=kt * e_out, bd=akk * e_out,
                kdg=kt * e_rem, bdg=akk * e_rem, gam=jnp.exp(g_tot))


def _rwkv_chains(preps, stricts, eye, blks, lane_lo, h_scr):
    L = lane_lo.shape[0]
    n = 2 * L
    n_pairs = RWKV_WIDTH // PAIR
    chains = [(z, p) for z in range(len(preps)) for p in range(n_pairs)]

    def stack(x):
        return jnp.concatenate([jnp.where(lane_lo, x, 0.0), jnp.where(lane_lo, 0.0, x)], axis=0)

    def part(name):
        return [stack(preps[z][name][:, p * PAIR:(p + 1) * PAIR]) for z, p in chains]

    kkg, rg, kd, bd, kdg, bdg, v = (part(s) for s in ("kkg", "rg", "kd", "bd", "kdg", "bdg", "v"))
    gram = [_bdot_nt(jnp.concatenate([a, b], axis=0), jnp.concatenate([c, d], axis=0))
            for a, b, c, d in zip(kkg, rg, kd, bd)]
    strict = [stricts[z] for z, _ in chains]
    incl = [jnp.logical_or(s, eye) for s in strict]
    akk_m = [jnp.where(s, g[:n, :n], 0.0) for s, g in zip(strict, gram)]
    akb_m = [jnp.where(s, g[:n, n:], 0.0) for s, g in zip(strict, gram)]
    ark_m = [jnp.where(s, g[n:, :n], 0.0) for s, g in zip(incl, gram)]
    arb_m = [jnp.where(s, g[n:, n:], 0.0) for s, g in zip(incl, gram)]
    t_inv = _tri_inverse_all(akb_m, eye, blks)
    av = [_bdot(jnp.concatenate([a, b], axis=0), x) for a, b, x in zip(akk_m, ark_m, v)]
    wu = [_bdot(t, jnp.concatenate([a, x[:n]], axis=1)) for t, a, x in zip(t_inv, kkg, av)]
    aw = [_bdot(a, x) for a, x in zip(arb_m, wu)]
    qe = [a - x[:, :PAIR] for a, x in zip(rg, aw)]
    yv = [a[n:] - x[:, PAIR:] for a, x in zip(av, aw)]
    c0 = [_bdot_tn(jnp.concatenate([a, b], axis=0), jnp.concatenate([x, -y[:, PAIR:]], axis=0))
          for a, b, x, y in zip(kdg, bdg, v, wu)]
    bw = [_bdot_tn(b, y[:, :PAIR]) for b, y in zip(bdg, wu)]
    gam = [jnp.broadcast_to(preps[z]["gam"][:, p * PAIR:(p + 1) * PAIR], (n, PAIR)) for z, p in chains]
    m = [jnp.where(eye, g, 0.0) - x for g, x in zip(gam, bw)]
    h0 = [h_scr[i] for i in range(len(chains))]
    y_s = [_bdot(a, h) + b for a, h, b in zip(qe, h0, yv)]
    h1 = [_bdot(a, h) + b for a, h, b in zip(m, h0, c0)]
    for i, h in enumerate(h1):
        h_scr[i] = h
    ys = [y[:L] + y[L:] for y in y_s]
    return [jnp.concatenate(ys[z * n_pairs:(z + 1) * n_pairs], axis=1) for z in range(len(preps))]


def _rwkv_kernel(fm_ref, fp_ref, fn_ref, bm_ref, bp_ref, bn_ref, taps_ref, tri_ref,
                 wup_ref, w0_ref, aup_ref, a0_ref, kk_ref, ka_ref, rk_ref, ones_ref,
                 yf_ref, yb_ref, bon_ref, h_scr):
    c = pl.program_id(1)
    n_chunks = pl.num_programs(1)
    L = fm_ref.shape[1]

    @pl.when(c == 0)
    def _():
        h_scr[...] = jnp.zeros_like(h_scr)

    taps = taps_ref[...]
    n = 2 * L
    ri = lax.broadcasted_iota(jnp.int32, (n, n), 0)
    ci = lax.broadcasted_iota(jnp.int32, (n, n), 1)
    eye = ri == ci
    blk = lambda s: (ri // s) == (ci // s)
    blk8, blk16, blk32, blk64 = blk(8), blk(16), blk(32), blk(64)
    lane_lo = lax.broadcasted_iota(jnp.int32, (L, PAIR), 1) < HEAD_DIM
    row_id = lax.broadcasted_iota(jnp.int32, (L, 1), 0)
    prm = (wup_ref, w0_ref, aup_ref, a0_ref, kk_ref, ka_ref, ones_ref)

    def shifted(m_ref, p_ref, n_ref, chunk):
        main = m_ref[0]
        prev_row = jnp.where(chunk > 0, p_ref[0][SUBLANES - 1:SUBLANES], 0.0)
        next_row = jnp.where(chunk < n_chunks - 1, n_ref[0][0:1], 0.0)
        up = jnp.where(row_id == 0, prev_row, pltpu.roll(main, 1, 0))
        dn = jnp.where(row_id == L - 1, next_row, pltpu.roll(main, L - 1, 0))
        return taps[0:1] * up + taps[1:2] * main + taps[2:3] * dn

    cb = n_chunks - 1 - c
    sh_f = shifted(fm_ref, fp_ref, fn_ref, c)
    sh_b = shifted(bm_ref, bp_ref, bn_ref, cb)
    prep_f = _rwkv_prep(0, sh_f, tri_ref, prm)
    prep_b = _rwkv_prep(1, sh_b, tri_ref, prm)
    stricts = (jnp.logical_and(blk64, ci < ri), jnp.logical_and(blk64, ci > ri))
    y_f, y_b = _rwkv_chains((prep_f, prep_b), stricts, eye, (blk8, blk16, blk32, blk64), lane_lo, h_scr)
    yf_ref[0] = y_f
    yb_ref[0] = y_b
    r_b, kt1_b, v_b = prep_b["r"], prep_b["kt"], prep_b["v"]

    W = RWKV_WIDTH
    ad_b = sh_b[:, 3 * W + LORA_RANK:3 * W + 2 * LORA_RANK]
    a0_b = _sigmoid(a0_ref[0] + _bdot(ad_b, aup_ref[0]))
    kt0_b = sh_b[:, W:2 * W] * (1.0 + (a0_b - 1.0) * ka_ref[...])
    rk_sum = r_b * (kt0_b + kt1_b) * rk_ref[...]
    rk_hi = rk_sum.astype(BF16)
    rk_lo = (rk_sum - rk_hi.astype(F32)).astype(BF16)
    ones = ones_ref[...]
    dots = (jnp.dot(rk_hi, ones, preferred_element_type=F32)
            + jnp.dot(rk_lo, ones, preferred_element_type=F32))
    bon_ref[0] = dots * v_b


def _rwkv_call(rin, taps, tri, w_up, w0, a_up, a0, k_k, k_a, r_k, ones):
    B, T, C = rin.shape
    L = CHUNK
    nC = T // L
    hb = L // SUBLANES
    n_hb = T // SUBLANES
    W = RWKV_WIDTH

    def main_f(b, c): return (b, c, 0)
    def prev_f(b, c): return (b, jnp.maximum(c * hb - 1, 0), 0)
    def next_f(b, c): return (b, jnp.minimum((c + 1) * hb, n_hb - 1), 0)
    def main_b(b, c): return (b, nC - 1 - c, 0)
    def prev_b(b, c): return (b, jnp.maximum((nC - 1 - c) * hb - 1, 0), 0)
    def next_b(b, c): return (b, jnp.minimum((nC - c) * hb, n_hb - 1), 0)

    const2 = lambda b, c: (0, 0)
    const3 = lambda b, c: (0, 0, 0)
    out_shape = tuple(jax.ShapeDtypeStruct((B, T, W), F32) for _ in range(3))
    return pl.pallas_call(
        _rwkv_kernel,
        name="rwkv7_chunked",
        grid=(B, nC),
        in_specs=[pl.BlockSpec((1, L, C), main_f),
                  pl.BlockSpec((1, SUBLANES, C), prev_f),
                  pl.BlockSpec((1, SUBLANES, C), next_f),
                  pl.BlockSpec((1, L, C), main_b),
                  pl.BlockSpec((1, SUBLANES, C), prev_b),
                  pl.BlockSpec((1, SUBLANES, C), next_b),
                  pl.BlockSpec((3, C), const2),
                  pl.BlockSpec((N_DIRS, L, L), const3),
                  pl.BlockSpec((N_DIRS, LORA_RANK, W), const3),
                  pl.BlockSpec((N_DIRS, 1, W), const3),
                  pl.BlockSpec((N_DIRS, LORA_RANK, W), const3),
                  pl.BlockSpec((N_DIRS, 1, W), const3),
                  pl.BlockSpec((1, W), const2),
                  pl.BlockSpec((1, W), const2),
                  pl.BlockSpec((1, W), const2),
                  pl.BlockSpec((W, W), const2)],
        out_specs=(pl.BlockSpec((1, L, W), main_f),
                   pl.BlockSpec((1, L, W), main_b),
                   pl.BlockSpec((1, L, W), main_b)),
        out_shape=out_shape,
        scratch_shapes=[pltpu.VMEM((N_DIRS * (W // PAIR), 2 * L, PAIR), F32)],
        compiler_params=pltpu.CompilerParams(dimension_semantics=("arbitrary", "arbitrary"),
                                             vmem_limit_bytes=VMEM_LIMIT),
    )(rin, rin, rin, rin, rin, rin, taps, tri, w_up, w0, a_up, a0, k_k, k_a, r_k, ones)


def _outproj_kernel(att_ref, yf_ref, yb_ref, bon_ref, grw_ref, x_ref, mod_ref, w_ref,
                    gnw_ref, gnb_ref, gpost_ref, ones_ref, o_ref):
    y = yf_ref[0] + yb_ref[0]
    ones = ones_ref[...]
    inv_n = 1.0 / HEAD_DIM
    mu = _bdot(y, ones) * inv_n
    d = y - mu
    var = _bdot(d * d, ones) * inv_n
    yn = d * lax.rsqrt(var + GN_EPS) * gnw_ref[...] + gnb_ref[...]
    g = grw_ref[0]
    rw = ((yn + bon_ref[0]) * (g * _sigmoid(g))).astype(BF16)
    out = (jnp.dot(att_ref[0], w_ref[:ATT_WIDTH, :], preferred_element_type=F32)
           + jnp.dot(rw, w_ref[ATT_WIDTH:, :], preferred_element_type=F32))
    ms = jnp.mean(out * out, axis=-1, keepdims=True)
    on = out * lax.rsqrt(ms + NORM_EPS) * gpost_ref[...]
    o_ref[0] = x_ref[0] + mod_ref[0] * on


def _outproj_call(att, yf, yb, bon, grw, x, mod3, w_out_bf, gn_w, gn_b, g_post, ones, tm):
    B, T, D = x.shape
    W = RWKV_WIDTH
    row = lambda t, b: (b, t, 0)
    const2 = lambda t, b: (0, 0)
    return pl.pallas_call(
        _outproj_kernel,
        name="out_proj",
        grid=(T // tm, B),
        in_specs=[pl.BlockSpec((1, tm, ATT_WIDTH), row),
                  pl.BlockSpec((1, tm, W), row),
                  pl.BlockSpec((1, tm, W), row),
                  pl.BlockSpec((1, tm, W), row),
                  pl.BlockSpec((1, tm, W), row),
                  pl.BlockSpec((1, tm, D), row),
                  pl.BlockSpec((1, 1, D), lambda t, b: (b, 0, 2)),
                  pl.BlockSpec((ATT_WIDTH + W, D), const2),
                  pl.BlockSpec((1, W), const2),
                  pl.BlockSpec((1, W), const2),
                  pl.BlockSpec((1, D), const2),
                  pl.BlockSpec((W, W), const2)],
        out_specs=pl.BlockSpec((1, tm, D), row),
        out_shape=jax.ShapeDtypeStruct((B, T, D), F32),
        compiler_params=pltpu.CompilerParams(dimension_semantics=("arbitrary", "arbitrary"),
                                             vmem_limit_bytes=VMEM_LIMIT),
    )(att, yf, yb, bon, grw, x, mod3, w_out_bf, gn_w, gn_b, g_post, ones)


def _pick_tile(T, target):
    t = min(T, target)
    while T % t:
        t //= 2
    return t


def kernel(x, c, w_ada, b_ada, g_pre, w_in, q_norm_g, k_norm_g, shift_taps, w_up, w0, a_up, a0,
           k_k, k_a, r_k, gn_w, gn_b, w_out, g_post):
    B, T, D = x.shape
    depth = w_ada.shape[0]
    assert T % CHUNK == 0 and T % GRID_W == 0
    tm = _pick_tile(T, 256)
    tq = _pick_tile(T, 256)
    cos, sin = _rope_tables(T)
    ones = _block_ones(RWKV_WIDTH)
    ti = jnp.arange(CHUNK)
    tri = jnp.stack([ti[None, :] <= ti[:, None], ti[None, :] >= ti[:, None]]).astype(BF16)
    for l in range(depth):
        mod = _ada_call(c, w_ada[l], b_ada[l])
        mod3 = mod.reshape(B, 1, 3 * D)
        qg = jnp.tile(q_norm_g[l], ATT_Q_HEADS).reshape(1, ATT_WIDTH)
        kg = jnp.tile(k_norm_g[l], ATT_KV_HEADS).reshape(1, ATT_KV_WIDTH)
        q, k, v, g_att, rin, g_rw = _inproj_call(
            x, mod3, g_pre[l].reshape(1, D), w_in[l].astype(BF16), qg, kg, cos, sin, ones, tm)
        att = _attn_call(q, k, v, g_att, tq)
        yf, yb, bon = _rwkv_call(
            rin, shift_taps[l], tri, w_up[l].astype(BF16), w0[l].reshape(N_DIRS, 1, RWKV_WIDTH),
            a_up[l].astype(BF16), a0[l].reshape(N_DIRS, 1, RWKV_WIDTH),
            k_k[l].reshape(1, RWKV_WIDTH), k_a[l].reshape(1, RWKV_WIDTH),
            r_k[l].reshape(1, RWKV_WIDTH), ones)
        x = _outproj_call(att, yf, yb, bon, g_rw, x, mod3, w_out[l].astype(BF16),
                          gn_w[l].reshape(1, RWKV_WIDTH), gn_b[l].reshape(1, RWKV_WIDTH),
                          g_post[l].reshape(1, D), ones, tm)
    return x
```

```python
import functools

import jax
import jax.numpy as jnp
from jax import lax
from jax.experimental import pallas as pl
from jax.experimental.pallas import tpu as pltpu

F32 = jnp.float32
BF16 = jnp.bfloat16

HEAD_DIM = 64
ATT_Q_HEADS = 8
ATT_KV_HEADS = 2
ATT_GROUP = ATT_Q_HEADS // ATT_KV_HEADS
ATT_WIDTH = ATT_Q_HEADS * HEAD_DIM
ATT_KV_WIDTH = ATT_KV_HEADS * HEAD_DIM
RWKV_HEADS = 8
RWKV_WIDTH = RWKV_HEADS * HEAD_DIM
LORA_RANK = 64
RWKV_SHIFT_WIDTH = 3 * RWKV_WIDTH + 2 * LORA_RANK
N_DIRS = 2
GRID_W = 64
ROPE_THETA = 10000.0
DECAY_SCALE = 0.6065306597126334
NORM_EPS = 1e-6
GN_EPS = 64e-5
L2_EPS = 1e-12

Q_OFF = 0
K_OFF = Q_OFF + ATT_WIDTH
V_OFF = K_OFF + ATT_KV_WIDTH
GATT_OFF = V_OFF + ATT_KV_WIDTH
RIN_OFF = GATT_OFF + ATT_WIDTH
GRW_OFF = RIN_OFF + RWKV_SHIFT_WIDTH
IN_WIDTH = GRW_OFF + RWKV_WIDTH

LANES = 128
SUBLANES = 8
CHUNK = 64
PAIR = 2 * HEAD_DIM
BF16_SUBLANES = 16
VT_ONES = BF16_SUBLANES
VT_KV = HEAD_DIM + VT_ONES
VT_ROWS = ATT_KV_HEADS * VT_KV
Q_SCALE = HEAD_DIM ** -0.5 * 1.4426950408889634
Q_HEAD_ORDER = tuple(g + kv * ATT_GROUP for g in range(ATT_GROUP) for kv in range(ATT_KV_HEADS))
VMEM_LIMIT = 56 * 1024 * 1024


def _bdot(a, b):
    return jnp.dot(a.astype(BF16), b.astype(BF16), preferred_element_type=F32)


def _bdot_nt(a, b):
    return lax.dot_general(a.astype(BF16), b.astype(BF16), (((1,), (1,)), ((), ())),
                           preferred_element_type=F32)


def _bdot_tn(a, b):
    return lax.dot_general(a.astype(BF16), b.astype(BF16), (((0,), (0,)), ((), ())),
                           preferred_element_type=F32)


def _sigmoid(x):
    return 1.0 / (1.0 + jnp.exp(-x))


def _ada_kernel(c_ref, w_ref, b_ref, o_ref):
    c = c_ref[...]
    ca = c * _sigmoid(c)
    o_ref[...] = jnp.dot(ca, w_ref[...], precision=lax.Precision.HIGHEST,
                         preferred_element_type=F32) + b_ref[...]


def _ada_call(c, w_ada, b_ada):
    B, D = c.shape
    n_out = w_ada.shape[1]
    tn = D
    return pl.pallas_call(
        _ada_kernel,
        name="ada_mod",
        grid=(n_out // tn,),
        in_specs=[pl.BlockSpec((B, D), lambda j: (0, 0)),
                  pl.BlockSpec((D, tn), lambda j: (0, j)),
                  pl.BlockSpec((1, tn), lambda j: (0, j))],
        out_specs=pl.BlockSpec((B, tn), lambda j: (0, j)),
        out_shape=jax.ShapeDtypeStruct((B, n_out), F32),
        compiler_params=pltpu.CompilerParams(dimension_semantics=("arbitrary",),
                                             vmem_limit_bytes=VMEM_LIMIT),
    )(c, w_ada, b_ada.reshape(1, n_out))


def _rope(x, cos, sin_signed, first_half):
    n = x.shape[1]
    partner = jnp.where(first_half, pltpu.roll(x, n - 16, 1), pltpu.roll(x, 16, 1))
    return x * cos + partner * sin_signed


def _inproj_kernel(x_ref, mod_ref, gpre_ref, w_ref, qg_ref, kg_ref, cos_ref, sin_ref, ones_ref,
                   q_ref, k_ref, vt_ref, gatt_ref, rin_ref, grw_ref):
    D = x_ref.shape[2]
    x = x_ref[0]
    ms = jnp.mean(x * x, axis=-1, keepdims=True)
    xn = x * lax.rsqrt(ms + NORM_EPS) * gpre_ref[...]
    mod = mod_ref[0]
    shift = mod[:, :D]
    scale = mod[:, D:2 * D]
    h = (xn * (1.0 + scale) + shift).astype(BF16)

    def proj(lo, width):
        return jnp.dot(h, w_ref[:, lo:lo + width], preferred_element_type=F32)

    cos = cos_ref[...]
    sin = sin_ref[...]
    lane = lax.broadcasted_iota(jnp.int32, cos.shape, 1)
    first_half = (lane % 32) < 16

    def head_norm(y, g):
        w = y.shape[1]
        msq = _bdot(y * y, ones_ref[:w, :w]) * (1.0 / HEAD_DIM)
        return y * lax.rsqrt(msq + NORM_EPS) * g

    q = head_norm(proj(Q_OFF, ATT_WIDTH), qg_ref[...])
    reps = ATT_WIDTH // LANES
    cos_q = jnp.concatenate([cos] * reps, axis=1)
    sin_q = jnp.concatenate([sin] * reps, axis=1)
    lane_q = lax.broadcasted_iota(jnp.int32, cos_q.shape, 1)
    fh_q = (lane_q % 32) < 16
    q = _rope(q, cos_q, sin_q, fh_q) * Q_SCALE
    q_ref[0] = q.astype(BF16)

    k = head_norm(proj(K_OFF, ATT_KV_WIDTH), kg_ref[...])
    k_ref[0] = _rope(k, cos, sin, first_half).astype(BF16)

    v = proj(V_OFF, ATT_KV_WIDTH).astype(BF16)
    eye = (lax.broadcasted_iota(jnp.int32, (ATT_KV_WIDTH, ATT_KV_WIDTH), 0)
           == lax.broadcasted_iota(jnp.int32, (ATT_KV_WIDTH, ATT_KV_WIDTH), 1)).astype(BF16)
    vt = _bdot_nt(eye, v).astype(BF16)
    ones_rows = jnp.ones((VT_ONES, vt.shape[1]), BF16)
    vt_ref[0] = jnp.concatenate([vt[:HEAD_DIM], ones_rows, vt[HEAD_DIM:], ones_rows], axis=0)
    gatt_ref[0] = proj(GATT_OFF, ATT_WIDTH).astype(BF16)
    rin_ref[0] = proj(RIN_OFF, RWKV_SHIFT_WIDTH)
    grw_ref[0] = proj(GRW_OFF, RWKV_WIDTH).astype(BF16)


def _rope_tables(T):
    rows = T // GRID_W
    row = jnp.repeat(jnp.arange(rows, dtype=F32), GRID_W)
    col = jnp.tile(jnp.arange(GRID_W, dtype=F32), rows)
    n_freq = HEAD_DIM // 4
    inv_freq = ROPE_THETA ** (-jnp.arange(n_freq, dtype=F32) / n_freq)
    ang_r = row[:, None] * inv_freq
    ang_c = col[:, None] * inv_freq
    cr, sr, cc, sc = jnp.cos(ang_r), jnp.sin(ang_r), jnp.cos(ang_c), jnp.sin(ang_c)
    cos = jnp.concatenate([cr, cr, cc, cc], axis=1)
    sin = jnp.concatenate([-sr, sr, -sc, sc], axis=1)
    reps = LANES // HEAD_DIM
    return jnp.tile(cos, (1, reps)), jnp.tile(sin, (1, reps))


def _block_ones(n):
    i = jnp.arange(n) // HEAD_DIM
    return (i[:, None] == i[None, :]).astype(BF16)


def _inproj_call(x, mod3, g_pre, w_in_bf, qg, kg, cos, sin, ones, tm):
    B, T, D = x.shape
    grid = (T // tm, B)
    row = lambda t, b: (b, t, 0)
    const2 = lambda t, b: (0, 0)
    out_shapes = (
        jax.ShapeDtypeStruct((B, T, ATT_WIDTH), BF16),
        jax.ShapeDtypeStruct((B, T, ATT_KV_WIDTH), BF16),
        jax.ShapeDtypeStruct((B, VT_ROWS, T), BF16),
        jax.ShapeDtypeStruct((B, T, ATT_WIDTH), BF16),
        jax.ShapeDtypeStruct((B, T, RWKV_SHIFT_WIDTH), F32),
        jax.ShapeDtypeStruct((B, T, RWKV_WIDTH), BF16),
    )
    out_specs = tuple(
        pl.BlockSpec((1, VT_ROWS, tm), lambda t, b: (b, 0, t)) if i == 2
        else pl.BlockSpec((1, tm, s.shape[2]), row) for i, s in enumerate(out_shapes))
    return pl.pallas_call(
        _inproj_kernel,
        name="in_proj",
        grid=grid,
        in_specs=[pl.BlockSpec((1, tm, D), row),
                  pl.BlockSpec((1, 1, mod3.shape[2]), lambda t, b: (b, 0, 0)),
                  pl.BlockSpec((1, D), const2),
                  pl.BlockSpec((D, IN_WIDTH), const2),
                  pl.BlockSpec((1, ATT_WIDTH), const2),
                  pl.BlockSpec((1, ATT_KV_WIDTH), const2),
                  pl.BlockSpec((tm, LANES), lambda t, b: (t, 0)),
                  pl.BlockSpec((tm, LANES), lambda t, b: (t, 0)),
                  pl.BlockSpec((ATT_WIDTH, ATT_WIDTH), const2)],
        out_specs=out_specs,
        out_shape=out_shapes,
        compiler_params=pltpu.CompilerParams(dimension_semantics=("arbitrary", "arbitrary"),
                                             vmem_limit_bytes=VMEM_LIMIT),
    )(x, mod3, g_pre, w_in_bf, qg, kg, cos, sin, ones)


def _attn_kernel(q_ref, k_ref, vt_ref, g_ref, o_ref):
    q = q_ref[0]
    k = k_ref[0]
    tq = q.shape[0]
    lane_lo = lax.broadcasted_iota(jnp.int32, (tq, PAIR), 1) < HEAD_DIM
    zero = jnp.zeros((tq, PAIR), BF16)
    def scores(j):
        qp = q[:, j * PAIR:(j + 1) * PAIR]
        qs = jnp.concatenate([jnp.where(lane_lo, qp, zero), jnp.where(lane_lo, zero, qp)], axis=0)
        return lax.dot_general(k, qs, (((1,), (1,)), ((), ())), preferred_element_type=F32)

    outs = []
    s_next = scores(0)
    for j in range(ATT_GROUP):
        s = s_next
        if j + 1 < ATT_GROUP:
            s_next = scores(j + 1)
        m = jnp.max(s, axis=0, keepdims=True)
        p = jnp.exp2(s - m).astype(BF16)
        for kv in range(ATT_KV_HEADS):
            vt = vt_ref[0, kv * VT_KV:(kv + 1) * VT_KV, :]
            o = jnp.dot(vt, p[:, kv * tq:(kv + 1) * tq], preferred_element_type=F32)
            outs.append(o[:HEAD_DIM] / o[HEAD_DIM:HEAD_DIM + 1])
    y = jnp.concatenate(outs, axis=0).T
    g = g_ref[0].astype(F32)
    o_ref[0] = (y * (g * _sigmoid(g))).astype(BF16)


def _attn_call(q, k, vt, g_att, tq):
    B, T, _ = q.shape
    return pl.pallas_call(
        _attn_kernel,
        name="gqa_attn",
        grid=(B, T // tq),
        in_specs=[pl.BlockSpec((1, tq, ATT_WIDTH), lambda b, i: (b, i, 0)),
                  pl.BlockSpec((1, T, ATT_KV_WIDTH), lambda b, i: (b, 0, 0)),
                  pl.BlockSpec((1, VT_ROWS, T), lambda b, i: (b, 0, 0)),
                  pl.BlockSpec((1, tq, ATT_WIDTH), lambda b, i: (b, i, 0))],
        out_specs=pl.BlockSpec((1, tq, ATT_WIDTH), lambda b, i: (b, i, 0)),
        out_shape=jax.ShapeDtypeStruct((B, T, ATT_WIDTH), BF16),
        compiler_params=pltpu.CompilerParams(dimension_semantics=("arbitrary", "arbitrary"),
                                             vmem_limit_bytes=VMEM_LIMIT),
    )(q, k, vt, g_att)


def _tri_inverse_all(a_list, eye, blks):
    blk8, blk16, blk32, blk64 = blks
    a8 = [jnp.where(blk8, a, 0.0) for a in a_list]
    a2 = [_bdot(x, x) for x in a8]
    a4 = [_bdot(x, x) for x in a2]
    t = [_bdot(eye - x, eye + y) for x, y in zip(a8, a2)]
    t = [_bdot(x, eye + y) for x, y in zip(t, a4)]
    for inner, outer in ((blk8, blk16), (blk16, blk32), (blk32, blk64)):
        off = jnp.logical_and(outer, jnp.logical_not(inner))
        xs = [_bdot(x, jnp.where(off, a, 0.0)) for x, a in zip(t, a_list)]
        t = [x - _bdot(y, x) for x, y in zip(t, xs)]
    return t


def _rwkv_prep(z, sh, tri_ref, prm):
    (wup_ref, w0_ref, aup_ref, a0_ref, kk_ref, ka_ref, ones_ref) = prm
    W = RWKV_WIDTH
    r = sh[:, 0:W]
    k = sh[:, W:2 * W]
    v = sh[:, 2 * W:3 * W]
    wd = sh[:, 3 * W:3 * W + LORA_RANK]
    ad = sh[:, 3 * W + LORA_RANK:3 * W + 2 * LORA_RANK]

    lw = -DECAY_SCALE * _sigmoid(w0_ref[z] + _bdot(jnp.tanh(wd), wup_ref[z]))
    a = _sigmoid(a0_ref[z] + _bdot(ad, aup_ref[z]))
    kkr = k * kk_ref[...]
    kk = kkr * lax.rsqrt(_bdot(kkr * kkr, ones_ref[...]) + L2_EPS)
    kt = k * (1.0 + (a - 1.0) * ka_ref[...])
    akk = a * kk

    lw_hi = lw.astype(BF16)
    lw_lo = (lw - lw_hi.astype(F32)).astype(BF16)
    tri = tri_ref[z]
    g = (jnp.dot(tri, lw_hi, preferred_element_type=F32)
         + jnp.dot(tri, lw_lo, preferred_element_type=F32))
    g_tot = jnp.sum(lw, axis=0, keepdims=True)
    e_out = jnp.exp(-g)
    e_rem = jnp.exp(g_tot - g)
    return dict(r=r, kt=kt, v=v,
                kkg=kk * jnp.exp(g - lw), rg=r * jnp.exp(g), kd=kt * e_out, bd=akk * e_out,
                kdg=kt * e_rem, bdg=akk * e_rem, gam=jnp.exp(g_tot))


def _rwkv_chains(preps, stricts, eye, blks, lane_lo, h_scr):
    L = lane_lo.shape[0]
    n = 2 * L
    n_pairs = RWKV_WIDTH // PAIR
    chains = [(z, p) for z in range(len(preps)) for p in range(n_pairs)]

    def stack(x):
        return jnp.concatenate([jnp.where(lane_lo, x, 0.0), jnp.where(lane_lo, 0.0, x)], axis=0)

    def part(name):
        return [stack(preps[z][name][:, p * PAIR:(p + 1) * PAIR]) for z, p in chains]

    kkg, rg, kd, bd, kdg, bdg, v = (part(s) for s in ("kkg", "rg", "kd", "bd", "kdg", "bdg", "v"))
    gram = [_bdot_nt(jnp.concatenate([a, b], axis=0), jnp.concatenate([c, d], axis=0))
            for a, b, c, d in zip(kkg, rg, kd, bd)]
    strict = [stricts[z] for z, _ in chains]
    incl = [jnp.logical_or(s, eye) for s in strict]
    akk_m = [jnp.where(s, g[:n, :n], 0.0) for s, g in zip(strict, gram)]
    akb_m = [jnp.where(s, g[:n, n:], 0.0) for s, g in zip(strict, gram)]
    ark_m = [jnp.where(s, g[n:, :n], 0.0) for s, g in zip(incl, gram)]
    arb_m = [jnp.where(s, g[n:, n:], 0.0) for s, g in zip(incl, gram)]
    t_inv = _tri_inverse_all(akb_m, eye, blks)
    av = [_bdot(jnp.concatenate([a, b], axis=0), x) for a, b, x in zip(akk_m, ark_m, v)]
    wu = [_bdot(t, jnp.concatenate([a, x[:n]], axis=1)) for t, a, x in zip(t_inv, kkg, av)]
    aw = [_bdot(a, x) for a, x in zip(arb_m, wu)]
    qe = [a - x[:, :PAIR] for a, x in zip(rg, aw)]
    yv = [a[n:] - x[:, PAIR:] for a, x in zip(av, aw)]
    c0 = [_bdot_tn(jnp.concatenate([a, b], axis=0), jnp.concatenate([x, -y[:, PAIR:]], axis=0))
          for a, b, x, y in zip(kdg, bdg, v, wu)]
    bw = [_bdot_tn(b, y[:, :PAIR]) for b, y in zip(bdg, wu)]
    gam = [jnp.broadcast_to(preps[z]["gam"][:, p * PAIR:(p + 1) * PAIR], (n, PAIR)) for z, p in chains]
    m = [jnp.where(eye, g, 0.0) - x for g, x in zip(gam, bw)]
    h0 = [h_scr[i] for i in range(len(chains))]
    y_s = [_bdot(a, h) + b for a, h, b in zip(qe, h0, yv)]
    h1 = [_bdot(a, h) + b for a, h, b in zip(m, h0, c0)]
    for i, h in enumerate(h1):
        h_scr[i] = h
    ys = [y[:L] + y[L:] for y in y_s]
    return [jnp.concatenate(ys[z * n_pairs:(z + 1) * n_pairs], axis=1) for z in range(len(preps))]


def _rwkv_kernel(fm_ref, fp_ref, fn_ref, bm_ref, bp_ref, bn_ref, taps_ref, tri_ref,
                 wup_ref, w0_ref, aup_ref, a0_ref, kk_ref, ka_ref, rk_ref, ones_ref,
                 yf_ref, yb_ref, bon_ref, h_scr):
    c = pl.program_id(1)
    n_chunks = pl.num_programs(1)
    L = fm_ref.shape[1]

    @pl.when(c == 0)
    def _():
        h_scr[...] = jnp.zeros_like(h_scr)

    taps = taps_ref[...]
    n = 2 * L
    ri = lax.broadcasted_iota(jnp.int32, (n, n), 0)
    ci = lax.broadcasted_iota(jnp.int32, (n, n), 1)
    eye = ri == ci
    blk = lambda s: (ri // s) == (ci // s)
    blk8, blk16, blk32, blk64 = blk(8), blk(16), blk(32), blk(64)
    lane_lo = lax.broadcasted_iota(jnp.int32, (L, PAIR), 1) < HEAD_DIM
    row_id = lax.broadcasted_iota(jnp.int32, (L, 1), 0)
    prm = (wup_ref, w0_ref, aup_ref, a0_ref, kk_ref, ka_ref, ones_ref)

    def shifted(m_ref, p_ref, n_ref, chunk):
        main = m_ref[0]
        prev_row = jnp.where(chunk > 0, p_ref[0][SUBLANES - 1:SUBLANES], 0.0)
        next_row = jnp.where(chunk < n_chunks - 1, n_ref[0][0:1], 0.0)
        up = jnp.where(row_id == 0, prev_row, pltpu.roll(main, 1, 0))
        dn = jnp.where(row_id == L - 1, next_row, pltpu.roll(main, L - 1, 0))
        return taps[0:1] * up + taps[1:2] * main + taps[2:3] * dn

    cb = n_chunks - 1 - c
    sh_f = shifted(fm_ref, fp_ref, fn_ref, c)
    sh_b = shifted(bm_ref, bp_ref, bn_ref, cb)
    prep_f = _rwkv_prep(0, sh_f, tri_ref, prm)
    prep_b = _rwkv_prep(1, sh_b, tri_ref, prm)
    stricts = (jnp.logical_and(blk64, ci < ri), jnp.logical_and(blk64, ci > ri))
    y_f, y_b = _rwkv_chains((prep_f, prep_b), stricts, eye, (blk8, blk16, blk32, blk64), lane_lo, h_scr)
    yf_ref[0] = y_f.astype(BF16)
    yb_ref[0] = y_b.astype(BF16)
    r_b, kt1_b, v_b = prep_b["r"], prep_b["kt"], prep_b["v"]

    W = RWKV_WIDTH
    ad_b = sh_b[:, 3 * W + LORA_RANK:3 * W + 2 * LORA_RANK]
    a0_b = _sigmoid(a0_ref[0] + _bdot(ad_b, aup_ref[0]))
    kt0_b = sh_b[:, W:2 * W] * (1.0 + (a0_b - 1.0) * ka_ref[...])
    rk_sum = r_b * (kt0_b + kt1_b) * rk_ref[...]
    rk_hi = rk_sum.astype(BF16)
    rk_lo = (rk_sum - rk_hi.astype(F32)).astype(BF16)
    ones = ones_ref[...]
    dots = (jnp.dot(rk_hi, ones, preferred_element_type=F32)
            + jnp.dot(rk_lo, ones, preferred_element_type=F32))
    bon_ref[0] = (dots * v_b).astype(BF16)


def _rwkv_call(rin, taps, tri, w_up, w0, a_up, a0, k_k, k_a, r_k, ones):
    B, T, C = rin.shape
    L = CHUNK
    nC = T // L
    hb = L // SUBLANES
    n_hb = T // SUBLANES
    W = RWKV_WIDTH

    def main_f(b, c): return (b, c, 0)
    def prev_f(b, c): return (b, jnp.maximum(c * hb - 1, 0), 0)
    def next_f(b, c): return (b, jnp.minimum((c + 1) * hb, n_hb - 1), 0)
    def main_b(b, c): return (b, nC - 1 - c, 0)
    def prev_b(b, c): return (b, jnp.maximum((nC - 1 - c) * hb - 1, 0), 0)
    def next_b(b, c): return (b, jnp.minimum((nC - c) * hb, n_hb - 1), 0)

    const2 = lambda b, c: (0, 0)
    const3 = lambda b, c: (0, 0, 0)
    out_shape = tuple(jax.ShapeDtypeStruct((B, T, W), BF16) for _ in range(3))
    return pl.pallas_call(
        _rwkv_kernel,
        name="rwkv7_chunked",
        grid=(B, nC),
        in_specs=[pl.BlockSpec((1, L, C), main_f),
                  pl.BlockSpec((1, SUBLANES, C), prev_f),
                  pl.BlockSpec((1, SUBLANES, C), next_f),
                  pl.BlockSpec((1, L, C), main_b),
                  pl.BlockSpec((1, SUBLANES, C), prev_b),
                  pl.BlockSpec((1, SUBLANES, C), next_b),
                  pl.BlockSpec((3, C), const2),
                  pl.BlockSpec((N_DIRS, L, L), const3),
                  pl.BlockSpec((N_DIRS, LORA_RANK, W), const3),
                  pl.BlockSpec((N_DIRS, 1, W), const3),
                  pl.BlockSpec((N_DIRS, LORA_RANK, W), const3),
                  pl.BlockSpec((N_DIRS, 1, W), const3),
                  pl.BlockSpec((1, W), const2),
                  pl.BlockSpec((1, W), const2),
                  pl.BlockSpec((1, W), const2),
                  pl.BlockSpec((W, W), const2)],
        out_specs=(pl.BlockSpec((1, L, W), main_f),
                   pl.BlockSpec((1, L, W), main_b),
                   pl.BlockSpec((1, L, W), main_b)),
        out_shape=out_shape,
        scratch_shapes=[pltpu.VMEM((N_DIRS * (W // PAIR), 2 * L, PAIR), F32)],
        compiler_params=pltpu.CompilerParams(dimension_semantics=("arbitrary", "arbitrary"),
                                             vmem_limit_bytes=VMEM_LIMIT),
    )(rin, rin, rin, rin, rin, rin, taps, tri, w_up, w0, a_up, a0, k_k, k_a, r_k, ones)


def _outproj_kernel(att_ref, yf_ref, yb_ref, bon_ref, grw_ref, x_ref, mod_ref, w_ref,
                    gnw_ref, gnb_ref, gpost_ref, ones_ref, o_ref):
    y = yf_ref[0].astype(F32) + yb_ref[0].astype(F32)
    ones = ones_ref[...]
    inv_n = 1.0 / HEAD_DIM
    mu = _bdot(y, ones) * inv_n
    d = y - mu
    var = _bdot(d * d, ones) * inv_n
    yn = d * lax.rsqrt(var + GN_EPS) * gnw_ref[...] + gnb_ref[...]
    g = grw_ref[0].astype(F32)
    rw = ((yn + bon_ref[0].astype(F32)) * (g * _sigmoid(g))).astype(BF16)
    out = (jnp.dot(att_ref[0], w_ref[:ATT_WIDTH, :], preferred_element_type=F32)
           + jnp.dot(rw, w_ref[ATT_WIDTH:, :], preferred_element_type=F32))
    ms = jnp.mean(out * out, axis=-1, keepdims=True)
    on = out * lax.rsqrt(ms + NORM_EPS) * gpost_ref[...]
    o_ref[0] = x_ref[0] + mod_ref[0] * on


def _outproj_call(att, yf, yb, bon, grw, x, mod3, w_out_bf, gn_w, gn_b, g_post, ones, tm):
    B, T, D = x.shape
    W = RWKV_WIDTH
    row = lambda t, b: (b, t, 0)
    const2 = lambda t, b: (0, 0)
    return pl.pallas_call(
        _outproj_kernel,
        name="out_proj",
        grid=(T // tm, B),
        in_specs=[pl.BlockSpec((1, tm, ATT_WIDTH), row),
                  pl.BlockSpec((1, tm, W), row),
                  pl.BlockSpec((1, tm, W), row),
                  pl.BlockSpec((1, tm, W), row),
                  pl.BlockSpec((1, tm, W), row),
                  pl.BlockSpec((1, tm, D), row),
                  pl.BlockSpec((1, 1, D), lambda t, b: (b, 0, 2)),
                  pl.BlockSpec((ATT_WIDTH + W, D), const2),
                  pl.BlockSpec((1, W), const2),
                  pl.BlockSpec((1, W), const2),
                  pl.BlockSpec((1, D), const2),
                  pl.BlockSpec((W, W), const2)],
        out_specs=pl.BlockSpec((1, tm, D), row),
        out_shape=jax.ShapeDtypeStruct((B, T, D), F32),
        compiler_params=pltpu.CompilerParams(dimension_semantics=("arbitrary", "arbitrary"),
                                             vmem_limit_bytes=VMEM_LIMIT),
    )(att, yf, yb, bon, grw, x, mod3, w_out_bf, gn_w, gn_b, g_post, ones)


def _pick_tile(T, target):
    t = min(T, target)
    while T % t:
        t //= 2
    return t


def _permute_heads(w, axis):
    blocks = jnp.split(w, ATT_Q_HEADS, axis=axis)
    return jnp.concatenate([blocks[h] for h in Q_HEAD_ORDER], axis=axis)


def kernel(x, c, w_ada, b_ada, g_pre, w_in, q_norm_g, k_norm_g, shift_taps, w_up, w0, a_up, a0,
           k_k, k_a, r_k, gn_w, gn_b, w_out, g_post):
    B, T, D = x.shape
    depth = w_ada.shape[0]
    assert T % CHUNK == 0 and T % GRID_W == 0
    tm = _pick_tile(T, 256)
    tq = _pick_tile(T, 256)
    cos, sin = _rope_tables(T)
    ones = _block_ones(RWKV_WIDTH)
    ti = jnp.arange(CHUNK)
    tri = jnp.stack([ti[None, :] <= ti[:, None], ti[None, :] >= ti[:, None]]).astype(BF16)
    for l in range(depth):
        mod = _ada_call(c, w_ada[l], b_ada[l])
        mod3 = mod.reshape(B, 1, 3 * D)
        qg = jnp.tile(q_norm_g[l], ATT_Q_HEADS).reshape(1, ATT_WIDTH)
        kg = jnp.tile(k_norm_g[l], ATT_KV_HEADS).reshape(1, ATT_KV_WIDTH)
        w_in_l = jnp.concatenate([_permute_heads(w_in[l][:, Q_OFF:K_OFF], 1),
                                  w_in[l][:, K_OFF:GATT_OFF],
                                  _permute_heads(w_in[l][:, GATT_OFF:RIN_OFF], 1),
                                  w_in[l][:, RIN_OFF:]], axis=1).astype(BF16)
        w_out_l = jnp.concatenate([_permute_heads(w_out[l][:ATT_WIDTH], 0),
                                   w_out[l][ATT_WIDTH:]], axis=0).astype(BF16)
        q, k, vt, g_att, rin, g_rw = _inproj_call(
            x, mod3, g_pre[l].reshape(1, D), w_in_l, qg, kg, cos, sin, ones, tm)
        att = _attn_call(q, k, vt, g_att, tq)
        yf, yb, bon = _rwkv_call(
            rin, shift_taps[l], tri, w_up[l].astype(BF16), w0[l].reshape(N_DIRS, 1, RWKV_WIDTH),
            a_up[l].astype(BF16), a0[l].reshape(N_DIRS, 1, RWKV_WIDTH),
            k_k[l].reshape(1, RWKV_WIDTH), k_a[l].reshape(1, RWKV_WIDTH),
            r_k[l].reshape(1, RWKV_WIDTH), ones)
        x = _outproj_call(att, yf, yb, bon, g_rw, x, mod3, w_out_l,
                          gn_w[l].reshape(1, RWKV_WIDTH), gn_b[l].reshape(1, RWKV_WIDTH),
                          g_post[l].reshape(1, D), ones, tm)
    return x
```

```python
import functools

import jax
import jax.numpy as jnp
from jax import lax
from jax.experimental import pallas as pl
from jax.experimental.pallas import tpu as pltpu

F32 = jnp.float32
BF16 = jnp.bfloat16

HEAD_DIM = 64
ATT_Q_HEADS = 8
ATT_KV_HEADS = 2
ATT_GROUP = ATT_Q_HEADS // ATT_KV_HEADS
ATT_WIDTH = ATT_Q_HEADS * HEAD_DIM
ATT_KV_WIDTH = ATT_KV_HEADS * HEAD_DIM
RWKV_HEADS = 8
RWKV_WIDTH = RWKV_HEADS * HEAD_DIM
LORA_RANK = 64
RWKV_SHIFT_WIDTH = 3 * RWKV_WIDTH + 2 * LORA_RANK
N_DIRS = 2
GRID_W = 64
ROPE_THETA = 10000.0
DECAY_SCALE = 0.6065306597126334
NORM_EPS = 1e-6
GN_EPS = 64e-5
L2_EPS = 1e-12

Q_OFF = 0
K_OFF = Q_OFF + ATT_WIDTH
V_OFF = K_OFF + ATT_KV_WIDTH
GATT_OFF = V_OFF + ATT_KV_WIDTH
RIN_OFF = GATT_OFF + ATT_WIDTH
GRW_OFF = RIN_OFF + RWKV_SHIFT_WIDTH
IN_WIDTH = GRW_OFF + RWKV_WIDTH

LANES = 128
SUBLANES = 8
CHUNK = 64
PAIR = 2 * HEAD_DIM
BF16_SUBLANES = 16
VT_ONES = BF16_SUBLANES
VT_KV = HEAD_DIM + VT_ONES
VT_ROWS = ATT_KV_HEADS * VT_KV
Q_SCALE = HEAD_DIM ** -0.5 * 1.4426950408889634
Q_HEAD_ORDER = tuple(g + kv * ATT_GROUP for g in range(ATT_GROUP) for kv in range(ATT_KV_HEADS))
VMEM_LIMIT = 56 * 1024 * 1024


def _bdot(a, b):
    return jnp.dot(a.astype(BF16), b.astype(BF16), preferred_element_type=F32)


def _bdot_nt(a, b):
    return lax.dot_general(a.astype(BF16), b.astype(BF16), (((1,), (1,)), ((), ())),
                           preferred_element_type=F32)


def _bdot_tn(a, b):
    return lax.dot_general(a.astype(BF16), b.astype(BF16), (((0,), (0,)), ((), ())),
                           preferred_element_type=F32)


def _sigmoid(x):
    return 1.0 / (1.0 + jnp.exp(-x))


def _ada_kernel(c_ref, w_ref, b_ref, o_ref):
    c = c_ref[...]
    ca = c * _sigmoid(c)
    o_ref[...] = jnp.dot(ca, w_ref[...], precision=lax.Precision.HIGHEST,
                         preferred_element_type=F32) + b_ref[...]


def _ada_call(c, w_ada, b_ada):
    B, D = c.shape
    n_out = w_ada.shape[1]
    tn = D
    return pl.pallas_call(
        _ada_kernel,
        name="ada_mod",
        grid=(n_out // tn,),
        in_specs=[pl.BlockSpec((B, D), lambda j: (0, 0)),
                  pl.BlockSpec((D, tn), lambda j: (0, j)),
                  pl.BlockSpec((1, tn), lambda j: (0, j))],
        out_specs=pl.BlockSpec((B, tn), lambda j: (0, j)),
        out_shape=jax.ShapeDtypeStruct((B, n_out), F32),
        compiler_params=pltpu.CompilerParams(dimension_semantics=("arbitrary",),
                                             vmem_limit_bytes=VMEM_LIMIT),
    )(c, w_ada, b_ada.reshape(1, n_out))


def _rope(x, cos, sin_signed, first_half):
    n = x.shape[1]
    partner = jnp.where(first_half, pltpu.roll(x, n - 16, 1), pltpu.roll(x, 16, 1))
    return x * cos + partner * sin_signed


def _inproj_kernel(x_ref, mod_ref, gpre_ref, w_ref, wq_ref, qg_ref, kg_ref, cos_ref, sin_ref, ones_ref,
                   q_ref, k_ref, vt_ref, gatt_ref, rin_ref, grw_ref):
    D = x_ref.shape[2]
    x = x_ref[0]
    ms = jnp.mean(x * x, axis=-1, keepdims=True)
    xn = x * lax.rsqrt(ms + NORM_EPS) * gpre_ref[...]
    mod = mod_ref[0]
    shift = mod[:, :D]
    scale = mod[:, D:2 * D]
    h = (xn * (1.0 + scale) + shift).astype(BF16)

    def proj(lo, width):
        return jnp.dot(h, w_ref[:, lo:lo + width], preferred_element_type=F32)

    cos = cos_ref[...]
    sin = sin_ref[...]
    lane = lax.broadcasted_iota(jnp.int32, cos.shape, 1)
    first_half = (lane % 32) < 16

    def head_norm(y, g):
        w = y.shape[1]
        msq = _bdot(y * y, ones_ref[:w, :w]) * (1.0 / HEAD_DIM)
        return y * lax.rsqrt(msq + NORM_EPS) * g

    q = head_norm(jnp.dot(h, wq_ref[...], preferred_element_type=F32), qg_ref[...])
    reps = ATT_WIDTH // LANES
    cos_q = jnp.concatenate([cos] * reps, axis=1)
    sin_q = jnp.concatenate([sin] * reps, axis=1)
    lane_q = lax.broadcasted_iota(jnp.int32, cos_q.shape, 1)
    fh_q = (lane_q % 32) < 16
    q = _rope(q, cos_q, sin_q, fh_q) * Q_SCALE
    q_ref[0] = q.astype(BF16)

    k = head_norm(proj(K_OFF, ATT_KV_WIDTH), kg_ref[...])
    k_ref[0] = _rope(k, cos, sin, first_half).astype(BF16)

    v = proj(V_OFF, ATT_KV_WIDTH).astype(BF16)
    eye = (lax.broadcasted_iota(jnp.int32, (ATT_KV_WIDTH, ATT_KV_WIDTH), 0)
           == lax.broadcasted_iota(jnp.int32, (ATT_KV_WIDTH, ATT_KV_WIDTH), 1)).astype(BF16)
    vt = _bdot_nt(eye, v).astype(BF16)
    ones_rows = jnp.ones((VT_ONES, vt.shape[1]), BF16)
    vt_ref[0] = jnp.concatenate([vt[:HEAD_DIM], ones_rows, vt[HEAD_DIM:], ones_rows], axis=0)
    gatt_ref[0] = proj(GATT_OFF, ATT_WIDTH).astype(BF16)
    rin_ref[0] = proj(RIN_OFF, RWKV_SHIFT_WIDTH)
    grw_ref[0] = proj(GRW_OFF, RWKV_WIDTH).astype(BF16)


def _rope_tables(T):
    rows = T // GRID_W
    row = jnp.repeat(jnp.arange(rows, dtype=F32), GRID_W)
    col = jnp.tile(jnp.arange(GRID_W, dtype=F32), rows)
    n_freq = HEAD_DIM // 4
    inv_freq = ROPE_THETA ** (-jnp.arange(n_freq, dtype=F32) / n_freq)
    ang_r = row[:, None] * inv_freq
    ang_c = col[:, None] * inv_freq
    cr, sr, cc, sc = jnp.cos(ang_r), jnp.sin(ang_r), jnp.cos(ang_c), jnp.sin(ang_c)
    cos = jnp.concatenate([cr, cr, cc, cc], axis=1)
    sin = jnp.concatenate([-sr, sr, -sc, sc], axis=1)
    reps = LANES // HEAD_DIM
    return jnp.tile(cos, (1, reps)), jnp.tile(sin, (1, reps))


def _block_ones(n):
    i = jnp.arange(n) // HEAD_DIM
    return (i[:, None] == i[None, :]).astype(BF16)


def _inproj_call(x, mod3, g_pre, w_in_bf, wq_bf, qg, kg, cos, sin, ones, tm):
    B, T, D = x.shape
    grid = (T // tm, B)
    row = lambda t, b: (b, t, 0)
    const2 = lambda t, b: (0, 0)
    out_shapes = (
        jax.ShapeDtypeStruct((B, T, ATT_WIDTH), BF16),
        jax.ShapeDtypeStruct((B, T, ATT_KV_WIDTH), BF16),
        jax.ShapeDtypeStruct((B, VT_ROWS, T), BF16),
        jax.ShapeDtypeStruct((B, T, ATT_WIDTH), BF16),
        jax.ShapeDtypeStruct((B, T, RWKV_SHIFT_WIDTH), F32),
        jax.ShapeDtypeStruct((B, T, RWKV_WIDTH), BF16),
    )
    out_specs = tuple(
        pl.BlockSpec((1, VT_ROWS, tm), lambda t, b: (b, 0, t)) if i == 2
        else pl.BlockSpec((1, tm, s.shape[2]), row) for i, s in enumerate(out_shapes))
    return pl.pallas_call(
        _inproj_kernel,
        name="in_proj",
        grid=grid,
        in_specs=[pl.BlockSpec((1, tm, D), row),
                  pl.BlockSpec((1, 1, mod3.shape[2]), lambda t, b: (b, 0, 0)),
                  pl.BlockSpec((1, D), const2),
                  pl.BlockSpec((D, IN_WIDTH), const2),
                  pl.BlockSpec((D, ATT_WIDTH), const2),
                  pl.BlockSpec((1, ATT_WIDTH), const2),
                  pl.BlockSpec((1, ATT_KV_WIDTH), const2),
                  pl.BlockSpec((tm, LANES), lambda t, b: (t, 0)),
                  pl.BlockSpec((tm, LANES), lambda t, b: (t, 0)),
                  pl.BlockSpec((ATT_WIDTH, ATT_WIDTH), const2)],
        out_specs=out_specs,
        out_shape=out_shapes,
        compiler_params=pltpu.CompilerParams(dimension_semantics=("arbitrary", "arbitrary"),
                                             vmem_limit_bytes=VMEM_LIMIT),
    )(x, mod3, g_pre, w_in_bf, wq_bf, qg, kg, cos, sin, ones)


def _attn_kernel(q_ref, k_ref, vt_ref, g_ref, o_ref):
    q = q_ref[0]
    k = k_ref[0]
    tq = q.shape[0]
    lane_lo = lax.broadcasted_iota(jnp.int32, (tq, PAIR), 1) < HEAD_DIM
    zero = jnp.zeros((tq, PAIR), BF16)
    def scores(j):
        qp = q[:, j * PAIR:(j + 1) * PAIR]
        qs = jnp.concatenate([jnp.where(lane_lo, qp, zero), jnp.where(lane_lo, zero, qp)], axis=0)
        return lax.dot_general(k, qs, (((1,), (1,)), ((), ())), preferred_element_type=F32)

    outs = [None] * ATT_Q_HEADS
    s_next = scores(0)
    for j in range(ATT_GROUP):
        s = s_next
        if j + 1 < ATT_GROUP:
            s_next = scores(j + 1)
        m = jnp.max(s, axis=0, keepdims=True)
        p = jnp.exp2(s - m).astype(BF16)
        for kv in range(ATT_KV_HEADS):
            vt = vt_ref[0, kv * VT_KV:(kv + 1) * VT_KV, :]
            o = jnp.dot(vt, p[:, kv * tq:(kv + 1) * tq], preferred_element_type=F32)
            outs[Q_HEAD_ORDER[j * ATT_KV_HEADS + kv]] = o[:HEAD_DIM] / o[HEAD_DIM:HEAD_DIM + 1]
    y = jnp.concatenate(outs, axis=0).T
    g = g_ref[0].astype(F32)
    o_ref[0] = (y * (g * _sigmoid(g))).astype(BF16)


def _attn_call(q, k, vt, g_att, tq):
    B, T, _ = q.shape
    return pl.pallas_call(
        _attn_kernel,
        name="gqa_attn",
        grid=(B, T // tq),
        in_specs=[pl.BlockSpec((1, tq, ATT_WIDTH), lambda b, i: (b, i, 0)),
                  pl.BlockSpec((1, T, ATT_KV_WIDTH), lambda b, i: (b, 0, 0)),
                  pl.BlockSpec((1, VT_ROWS, T), lambda b, i: (b, 0, 0)),
                  pl.BlockSpec((1, tq, ATT_WIDTH), lambda b, i: (b, i, 0))],
        out_specs=pl.BlockSpec((1, tq, ATT_WIDTH), lambda b, i: (b, i, 0)),
        out_shape=jax.ShapeDtypeStruct((B, T, ATT_WIDTH), BF16),
        compiler_params=pltpu.CompilerParams(dimension_semantics=("arbitrary", "arbitrary"),
                                             vmem_limit_bytes=VMEM_LIMIT),
    )(q, k, vt, g_att)


CHUNKS_PER_STEP = 2
N_PAIRS = RWKV_WIDTH // PAIR


def _later_rows(x, s, odd):
    first = s if odd else 0
    return jnp.concatenate([x[b:b + s] for b in range(first, x.shape[0], 2 * s)], axis=0)


def _merge_rows(keep, new, s, odd):
    out = []
    for i, b in enumerate(range(0, keep.shape[0], 2 * s)):
        lo, hi = keep[b:b + s], keep[b + s:b + 2 * s]
        blk = new[i * s:(i + 1) * s]
        out += [lo, blk] if odd else [blk, hi]
    return jnp.concatenate(out, axis=0)


INVERSE_STAGES = 9


def _inverse_stages(a_list, eye, blks, odds, out):
    blk8, blk16, blk32, blk64 = blks
    a8 = [jnp.where(blk8, a, 0.0) for a in a_list]
    a2 = [_bdot(x, x) for x in a8]
    yield
    a4 = [_bdot(x, x) for x in a2]
    t = [_bdot(eye - x, eye + y) for x, y in zip(a8, a2)]
    yield
    t = [_bdot(x, eye + y) for x, y in zip(t, a4)]
    for inner, outer, s in ((blk8, blk16, 8), (blk16, blk32, 16), (blk32, blk64, 32)):
        yield
        off = jnp.logical_and(outer, jnp.logical_not(inner))
        xs = [_bdot(_later_rows(x, s, o), jnp.where(off, a, 0.0)) for x, a, o in zip(t, a_list, odds)]
        yield
        upd = [_later_rows(x, s, o) - _bdot(y, x) for x, y, o in zip(t, xs, odds)]
        t = [_merge_rows(x, u, s, o) for x, u, o in zip(t, upd, odds)]
    out.extend(t)


def _rwkv_kernel(fm_ref, fp_ref, fn_ref, bm_ref, bp_ref, bn_ref, taps_ref, tri_ref,
                 wup_ref, w0_ref, aup_ref, a0_ref, kk_ref, ka_ref, rk_ref, ones_ref,
                 yf_ref, yb_ref, bon_ref,
                 h_scr, x1_scr, x2_scr, x3_scr, v_scr, gam_scr):
    s = pl.program_id(1)

    @pl.when(s == 0)
    def _():
        for ref in (h_scr, x1_scr, x2_scr, x3_scr, v_scr, gam_scr):
            ref[...] = jnp.zeros_like(ref)

    n_blocks = pl.num_programs(1) - 1
    SL = fm_ref.shape[1]
    L = CHUNK
    cps = SL // L
    W = RWKV_WIDTH
    n = 2 * L
    n_chains = x1_scr.shape[0]

    ri = lax.broadcasted_iota(jnp.int32, (n, n), 0)
    ci = lax.broadcasted_iota(jnp.int32, (n, n), 1)
    eye = ri == ci
    blk = lambda size: (ri // size) == (ci // size)
    blk8, blk16, blk32, blk64 = blk(8), blk(16), blk(32), blk(64)
    stricts = (jnp.logical_and(blk64, ci < ri), jnp.logical_and(blk64, ci > ri))
    chains = [(z, j, p) for z in range(N_DIRS) for j in range(cps) for p in range(N_PAIRS)]
    assert len(chains) == n_chains
    strict = [stricts[z] for z, _, _ in chains]
    incl = [jnp.logical_or(m, eye) for m in strict]
    odds = [z == 0 for z, _, _ in chains]
    blks = (blk8, blk16, blk32, blk64)

    x1 = [x1_scr[i] for i in range(n_chains)]
    x2 = [x2_scr[i] for i in range(n_chains)]
    x3 = [x3_scr[i] for i in range(n_chains)]
    v_s = [v_scr[i] for i in range(n_chains)]
    gam_rows = [gam_scr[i][0:1] for i in range(n_chains)]
    gram = [_bdot_nt(a, b) for a, b in zip(x1, x2)]
    akk_m = [jnp.where(m, g[:n, :n], 0.0) for m, g in zip(strict, gram)]
    akb_m = [jnp.where(m, g[:n, n:], 0.0) for m, g in zip(strict, gram)]
    ark_m = [jnp.where(m, g[n:, :n], 0.0) for m, g in zip(incl, gram)]
    arb_m = [jnp.where(m, g[n:, n:], 0.0) for m, g in zip(incl, gram)]

    t_inv = []
    inverse = _inverse_stages(akb_m, eye, blks, odds, t_inv)
    av = [None] * n_chains

    def apply_values(lo, hi):
        for i in range(lo, hi):
            av[i] = _bdot(jnp.concatenate([akk_m[i], ark_m[i]], axis=0), v_s[i])

    next(inverse, None)
    next(inverse, None)

    taps = taps_ref[...]
    row8 = lax.broadcasted_iota(jnp.int32, (SUBLANES, 1), 0)

    def shifted(m_ref, p_ref, n_ref, block):
        main = m_ref[0]
        prev_row = jnp.where(block > 0, p_ref[0][SUBLANES - 1:SUBLANES], 0.0)
        next_row = jnp.where(block < n_blocks - 1, n_ref[0][0:1], 0.0)
        up = pltpu.roll(main, 1, 0)
        dn = pltpu.roll(main, SL - 1, 0)
        up = jnp.concatenate([jnp.where(row8 == 0, prev_row, up[:SUBLANES]), up[SUBLANES:]], axis=0)
        dn = jnp.concatenate([dn[:SL - SUBLANES],
                              jnp.where(row8 == SUBLANES - 1, next_row, dn[SL - SUBLANES:])], axis=0)
        return taps[0:1] * up + taps[1:2] * main + taps[2:3] * dn

    cf = jnp.minimum(s, n_blocks - 1)
    cb = jnp.maximum(n_blocks - 1 - s, 0)
    sh = (shifted(fm_ref, fp_ref, fn_ref, cf), shifted(bm_ref, bp_ref, bn_ref, cb))
    r = [x[:, 0:W] for x in sh]
    k = [x[:, W:2 * W] for x in sh]
    v = [x[:, 2 * W:3 * W] for x in sh]
    wd = [x[:, 3 * W:3 * W + LORA_RANK] for x in sh]
    ad = [x[:, 3 * W + LORA_RANK:3 * W + 2 * LORA_RANK] for x in sh]
    lora_w = [_bdot(jnp.tanh(wd[z]), wup_ref[z]) for z in range(N_DIRS)]
    lora_a = [_bdot(ad[z], aup_ref[z]) for z in range(N_DIRS)]
    lora_a0b = _bdot(ad[1], aup_ref[0])
    kkr = [x * kk_ref[...] for x in k]
    kk_ss = [_bdot(x * x, ones_ref[...]) for x in kkr]

    next(inverse, None)
    next(inverse, None)

    lw = [-DECAY_SCALE * _sigmoid(w0_ref[z] + lora_w[z]) for z in range(N_DIRS)]
    a = [_sigmoid(a0_ref[z] + lora_a[z]) for z in range(N_DIRS)]
    kk = [x * lax.rsqrt(y + L2_EPS) for x, y in zip(kkr, kk_ss)]
    kt = [k[z] * (1.0 + (a[z] - 1.0) * ka_ref[...]) for z in range(N_DIRS)]
    akk = [a[z] * kk[z] for z in range(N_DIRS)]
    lw_hi = [x.astype(BF16) for x in lw]
    lw_lo = [(x - y.astype(F32)).astype(BF16) for x, y in zip(lw, lw_hi)]
    g = [jnp.dot(tri_ref[z], lw_hi[z], preferred_element_type=F32)
         + jnp.dot(tri_ref[z], lw_lo[z], preferred_element_type=F32) for z in range(N_DIRS)]
    a0_b = _sigmoid(a0_ref[0] + lora_a0b)
    kt0_b = k[1] * (1.0 + (a0_b - 1.0) * ka_ref[...])
    rk_sum = r[1] * (kt0_b + kt[1]) * rk_ref[...]
    rk_hi = rk_sum.astype(BF16)
    rk_lo = (rk_sum - rk_hi.astype(F32)).astype(BF16)
    bonus_dots = (jnp.dot(rk_hi, ones_ref[...], preferred_element_type=F32)
                  + jnp.dot(rk_lo, ones_ref[...], preferred_element_type=F32))

    quarter = n_chains // 4
    for part in range(4):
        next(inverse, None)
        apply_values(part * quarter, (part + 1) * quarter)
    next(inverse, None)
    assert len(t_inv) == n_chains
    wu = [_bdot(t, jnp.concatenate([p[:n], q[:n].astype(BF16)], axis=1))
          for t, p, q in zip(t_inv, x1, av)]
    aw = [_bdot(p, q) for p, q in zip(arb_m, wu)]
    qe = [p[n:].astype(F32) - q[:, :PAIR] for p, q in zip(x1, aw)]
    yv = [p[n:] - q[:, PAIR:] for p, q in zip(av, aw)]
    c0 = [_bdot_tn(p, jnp.concatenate([q, (-u[:, PAIR:]).astype(BF16)], axis=0))
          for p, q, u in zip(x3, v_s, wu)]
    bw = [_bdot_tn(p[n:], u[:, :PAIR]) for p, u in zip(x3, wu)]
    m_c = [jnp.where(eye, jnp.broadcast_to(g, (n, PAIR)), 0.0) - q for g, q in zip(gam_rows, bw)]

    lanes = [(z, p) for z in range(N_DIRS) for p in range(N_PAIRS)]
    h = [h_scr[z * N_PAIRS + p] for z, p in lanes]
    ys = [None] * n_chains
    for step in range(cps):
        idx = [chains.index((z, step if z == 0 else cps - 1 - step, p)) for z, p in lanes]
        y_s = [_bdot(qe[i], hh) + yv[i] for i, hh in zip(idx, h)]
        h = [_bdot(m_c[i], hh) + c0[i] for i, hh in zip(idx, h)]
        for i, y in zip(idx, y_s):
            ys[i] = y[:L] + y[L:]
    for (z, p), hh in zip(lanes, h):
        h_scr[z * N_PAIRS + p] = hh

    def assemble(z):
        return jnp.concatenate(
            [jnp.concatenate([ys[chains.index((z, j, p))] for p in range(N_PAIRS)], axis=1)
             for j in range(cps)], axis=0)

    yf_ref[0] = assemble(0).astype(BF16)
    yb_ref[0] = assemble(1).astype(BF16)

    bon_ref[0] = (bonus_dots * v[1]).astype(BF16)
    lane_lo = lax.broadcasted_iota(jnp.int32, (L, PAIR), 1) < HEAD_DIM
    zero = jnp.zeros((L, PAIR), BF16)

    def stack(x, j, p):
        xp = x[j * L:(j + 1) * L, p * PAIR:(p + 1) * PAIR]
        return jnp.concatenate([jnp.where(lane_lo, xp, zero), jnp.where(lane_lo, zero, xp)], axis=0)

    for z in range(N_DIRS):
        g_tot = [jnp.sum(lw[z][j * L:(j + 1) * L], axis=0, keepdims=True) for j in range(cps)]
        g_tot_rows = jnp.concatenate([jnp.broadcast_to(x, (L, W)) for x in g_tot], axis=0)
        e_out = jnp.exp(-g[z])
        e_rem = jnp.exp(g_tot_rows - g[z])
        kkg = (kk[z] * jnp.exp(g[z] - lw[z])).astype(BF16)
        rg = (r[z] * jnp.exp(g[z])).astype(BF16)
        kd = (kt[z] * e_out).astype(BF16)
        bd = (akk[z] * e_out).astype(BF16)
        kdg = (kt[z] * e_rem).astype(BF16)
        bdg = (akk[z] * e_rem).astype(BF16)
        v_bf = v[z].astype(BF16)
        for j in range(cps):
            gam = jnp.broadcast_to(jnp.exp(g_tot[j]), (SUBLANES, W))
            for p in range(N_PAIRS):
                i = chains.index((z, j, p))
                x1_scr[i] = jnp.concatenate([stack(kkg, j, p), stack(rg, j, p)], axis=0)
                x2_scr[i] = jnp.concatenate([stack(kd, j, p), stack(bd, j, p)], axis=0)
                x3_scr[i] = jnp.concatenate([stack(kdg, j, p), stack(bdg, j, p)], axis=0)
                v_scr[i] = stack(v_bf, j, p)
                gam_scr[i] = gam[:, p * PAIR:(p + 1) * PAIR]


def _rwkv_call(rin, taps, tri, w_up, w0, a_up, a0, k_k, k_a, r_k, ones):
    B, T, C = rin.shape
    L = CHUNK * CHUNKS_PER_STEP
    nC = T // L
    hb = L // SUBLANES
    n_hb = T // SUBLANES
    W = RWKV_WIDTH
    n_chains = N_DIRS * CHUNKS_PER_STEP * N_PAIRS

    def prep_f(s): return jnp.minimum(s, nC - 1)
    def prep_b(s): return jnp.maximum(nC - 1 - s, 0)
    def main_f(b, s): return (b, prep_f(s), 0)
    def prev_f(b, s): return (b, jnp.maximum(prep_f(s) * hb - 1, 0), 0)
    def next_f(b, s): return (b, jnp.minimum((prep_f(s) + 1) * hb, n_hb - 1), 0)
    def main_b(b, s): return (b, prep_b(s), 0)
    def prev_b(b, s): return (b, jnp.maximum(prep_b(s) * hb - 1, 0), 0)
    def next_b(b, s): return (b, jnp.minimum((prep_b(s) + 1) * hb, n_hb - 1), 0)
    def out_f(b, s): return (b, jnp.maximum(s - 1, 0), 0)
    def out_b(b, s): return (b, jnp.minimum(nC - s, nC - 1), 0)

    const2 = lambda b, s: (0, 0)
    const3 = lambda b, s: (0, 0, 0)
    out_shape = tuple(jax.ShapeDtypeStruct((B, T, W), BF16) for _ in range(3))
    n = 2 * CHUNK
    return pl.pallas_call(
        _rwkv_kernel,
        name="rwkv7_chunked",
        grid=(B, nC + 1),
        in_specs=[pl.BlockSpec((1, L, C), main_f),
                  pl.BlockSpec((1, SUBLANES, C), prev_f),
                  pl.BlockSpec((1, SUBLANES, C), next_f),
                  pl.BlockSpec((1, L, C), main_b),
                  pl.BlockSpec((1, SUBLANES, C), prev_b),
                  pl.BlockSpec((1, SUBLANES, C), next_b),
                  pl.BlockSpec((3, C), const2),
                  pl.BlockSpec((N_DIRS, L, L), const3),
                  pl.BlockSpec((N_DIRS, LORA_RANK, W), const3),
                  pl.BlockSpec((N_DIRS, 1, W), const3),
                  pl.BlockSpec((N_DIRS, LORA_RANK, W), const3),
                  pl.BlockSpec((N_DIRS, 1, W), const3),
                  pl.BlockSpec((1, W), const2),
                  pl.BlockSpec((1, W), const2),
                  pl.BlockSpec((1, W), const2),
                  pl.BlockSpec((W, W), const2)],
        out_specs=(pl.BlockSpec((1, L, W), out_f),
                   pl.BlockSpec((1, L, W), out_b),
                   pl.BlockSpec((1, L, W), main_b)),
        out_shape=out_shape,
        scratch_shapes=[pltpu.VMEM((N_DIRS * N_PAIRS, n, PAIR), F32),
                        pltpu.VMEM((n_chains, 2 * n, PAIR), BF16),
                        pltpu.VMEM((n_chains, 2 * n, PAIR), BF16),
                        pltpu.VMEM((n_chains, 2 * n, PAIR), BF16),
                        pltpu.VMEM((n_chains, n, PAIR), BF16),
                        pltpu.VMEM((n_chains, SUBLANES, PAIR), F32)],
        compiler_params=pltpu.CompilerParams(dimension_semantics=("arbitrary", "arbitrary"),
                                             vmem_limit_bytes=VMEM_LIMIT),
    )(rin, rin, rin, rin, rin, rin, taps, tri, w_up, w0, a_up, a0, k_k, k_a, r_k, ones)


def _outproj_kernel(att_ref, yf_ref, yb_ref, bon_ref, grw_ref, x_ref, mod_ref, w_ref,
                    gnw_ref, gnb_ref, gpost_ref, ones_ref, o_ref):
    y = yf_ref[0].astype(F32) + yb_ref[0].astype(F32)
    ones = ones_ref[...]
    inv_n = 1.0 / HEAD_DIM
    mu = _bdot(y, ones) * inv_n
    d = y - mu
    var = _bdot(d * d, ones) * inv_n
    yn = d * lax.rsqrt(var + GN_EPS) * gnw_ref[...] + gnb_ref[...]
    g = grw_ref[0].astype(F32)
    rw = ((yn + bon_ref[0].astype(F32)) * (g * _sigmoid(g))).astype(BF16)
    out = (jnp.dot(att_ref[0], w_ref[:ATT_WIDTH, :], preferred_element_type=F32)
           + jnp.dot(rw, w_ref[ATT_WIDTH:, :], preferred_element_type=F32))
    ms = jnp.mean(out * out, axis=-1, keepdims=True)
    on = out * lax.rsqrt(ms + NORM_EPS) * gpost_ref[...]
    o_ref[0] = x_ref[0] + mod_ref[0] * on


def _outproj_call(att, yf, yb, bon, grw, x, mod3, w_out_bf, gn_w, gn_b, g_post, ones, tm):
    B, T, D = x.shape
    W = RWKV_WIDTH
    row = lambda t, b: (b, t, 0)
    const2 = lambda t, b: (0, 0)
    return pl.pallas_call(
        _outproj_kernel,
        name="out_proj",
        grid=(T // tm, B),
        in_specs=[pl.BlockSpec((1, tm, ATT_WIDTH), row),
                  pl.BlockSpec((1, tm, W), row),
                  pl.BlockSpec((1, tm, W), row),
                  pl.BlockSpec((1, tm, W), row),
                  pl.BlockSpec((1, tm, W), row),
                  pl.BlockSpec((1, tm, D), row),
                  pl.BlockSpec((1, 1, D), lambda t, b: (b, 0, 2)),
                  pl.BlockSpec((ATT_WIDTH + W, D), const2),
                  pl.BlockSpec((1, W), const2),
                  pl.BlockSpec((1, W), const2),
                  pl.BlockSpec((1, D), const2),
                  pl.BlockSpec((W, W), const2)],
        out_specs=pl.BlockSpec((1, tm, D), row),
        out_shape=jax.ShapeDtypeStruct((B, T, D), F32),
        compiler_params=pltpu.CompilerParams(dimension_semantics=("arbitrary", "arbitrary"),
                                             vmem_limit_bytes=VMEM_LIMIT),
    )(att, yf, yb, bon, grw, x, mod3, w_out_bf, gn_w, gn_b, g_post, ones)


def _pick_tile(T, target):
    t = min(T, target)
    while T % t:
        t //= 2
    return t


def _permute_heads(w, axis):
    blocks = jnp.split(w, ATT_Q_HEADS, axis=axis)
    return jnp.concatenate([blocks[h] for h in Q_HEAD_ORDER], axis=axis)


def kernel(x, c, w_ada, b_ada, g_pre, w_in, q_norm_g, k_norm_g, shift_taps, w_up, w0, a_up, a0,
           k_k, k_a, r_k, gn_w, gn_b, w_out, g_post):
    B, T, D = x.shape
    depth = w_ada.shape[0]
    assert T % (CHUNK * CHUNKS_PER_STEP) == 0 and T % GRID_W == 0
    tm = _pick_tile(T, 256)
    tq = _pick_tile(T, 256)
    cos, sin = _rope_tables(T)
    ones = _block_ones(RWKV_WIDTH)
    ti = jnp.arange(CHUNK * CHUNKS_PER_STEP)
    same_chunk = (ti[None, :] // CHUNK) == (ti[:, None] // CHUNK)
    tri = jnp.stack([same_chunk & (ti[None, :] <= ti[:, None]),
                     same_chunk & (ti[None, :] >= ti[:, None])]).astype(BF16)
    for l in range(depth):
        mod = _ada_call(c, w_ada[l], b_ada[l])
        mod3 = mod.reshape(B, 1, 3 * D)
        qg = jnp.tile(q_norm_g[l], ATT_Q_HEADS).reshape(1, ATT_WIDTH)
        kg = jnp.tile(k_norm_g[l], ATT_KV_HEADS).reshape(1, ATT_KV_WIDTH)
        wq = _permute_heads(w_in[l][:, Q_OFF:K_OFF], 1).astype(BF16)
        q, k, vt, g_att, rin, g_rw = _inproj_call(
            x, mod3, g_pre[l].reshape(1, D), w_in[l].astype(BF16), wq, qg, kg, cos, sin, ones, tm)
        att = _attn_call(q, k, vt, g_att, tq)
        yf, yb, bon = _rwkv_call(
            rin, shift_taps[l], tri, w_up[l].astype(BF16), w0[l].reshape(N_DIRS, 1, RWKV_WIDTH),
            a_up[l].astype(BF16), a0[l].reshape(N_DIRS, 1, RWKV_WIDTH),
            k_k[l].reshape(1, RWKV_WIDTH), k_a[l].reshape(1, RWKV_WIDTH),
            r_k[l].reshape(1, RWKV_WIDTH), ones)
        x = _outproj_call(att, yf, yb, bon, g_rw, x, mod3, w_out[l].astype(BF16),
                          gn_w[l].reshape(1, RWKV_WIDTH), gn_b[l].reshape(1, RWKV_WIDTH),
                          g_post[l].reshape(1, D), ones, tm)
    return x
```

```python
import functools

import jax
import jax.numpy as jnp
from jax import lax
from jax.experimental import pallas as pl
from jax.experimental.pallas import tpu as pltpu

F32 = jnp.float32
BF16 = jnp.bfloat16

HEAD_DIM = 64
ATT_Q_HEADS = 8
ATT_KV_HEADS = 2
ATT_GROUP = ATT_Q_HEADS // ATT_KV_HEADS
ATT_WIDTH = ATT_Q_HEADS * HEAD_DIM
ATT_KV_WIDTH = ATT_KV_HEADS * HEAD_DIM
RWKV_HEADS = 8
RWKV_WIDTH = RWKV_HEADS * HEAD_DIM
LORA_RANK = 64
RWKV_SHIFT_WIDTH = 3 * RWKV_WIDTH + 2 * LORA_RANK
N_DIRS = 2
GRID_W = 64
ROPE_THETA = 10000.0
DECAY_SCALE = 0.6065306597126334
NORM_EPS = 1e-6
GN_EPS = 64e-5
L2_EPS = 1e-12

Q_OFF = 0
K_OFF = Q_OFF + ATT_WIDTH
V_OFF = K_OFF + ATT_KV_WIDTH
GATT_OFF = V_OFF + ATT_KV_WIDTH
RIN_OFF = GATT_OFF + ATT_WIDTH
GRW_OFF = RIN_OFF + RWKV_SHIFT_WIDTH
IN_WIDTH = GRW_OFF + RWKV_WIDTH

LANES = 128
SUBLANES = 8
CHUNK = 64
PAIR = 2 * HEAD_DIM
BF16_SUBLANES = 16
VT_ONES = BF16_SUBLANES
VT_KV = HEAD_DIM + VT_ONES
VT_ROWS = ATT_KV_HEADS * VT_KV
Q_SCALE = HEAD_DIM ** -0.5 * 1.4426950408889634
Q_HEAD_ORDER = tuple(g + kv * ATT_GROUP for g in range(ATT_GROUP) for kv in range(ATT_KV_HEADS))
VMEM_LIMIT = 56 * 1024 * 1024
IN_PROJ_ROWS = 256
OUT_PROJ_ROWS = 512
ATTN_QUERY_ROWS = 512


def _bdot(a, b):
    return jnp.dot(a.astype(BF16), b.astype(BF16), preferred_element_type=F32)


def _bdot_nt(a, b):
    return lax.dot_general(a.astype(BF16), b.astype(BF16), (((1,), (1,)), ((), ())),
                           preferred_element_type=F32)


def _bdot_tn(a, b):
    return lax.dot_general(a.astype(BF16), b.astype(BF16), (((0,), (0,)), ((), ())),
                           preferred_element_type=F32)


def _sigmoid(x):
    return 1.0 / (1.0 + jnp.exp(-x))


def _ada_kernel(c_ref, w_ref, b_ref, o_ref):
    c = c_ref[...]
    ca = c * _sigmoid(c)
    o_ref[...] = jnp.dot(ca, w_ref[...], precision=lax.Precision.HIGHEST,
                         preferred_element_type=F32) + b_ref[...]


def _ada_call(c, w_ada, b_ada):
    B, D = c.shape
    n_out = w_ada.shape[1]
    tn = D
    return pl.pallas_call(
        _ada_kernel,
        name="ada_mod",
        grid=(n_out // tn,),
        in_specs=[pl.BlockSpec((B, D), lambda j: (0, 0)),
                  pl.BlockSpec((D, tn), lambda j: (0, j)),
                  pl.BlockSpec((1, tn), lambda j: (0, j))],
        out_specs=pl.BlockSpec((B, tn), lambda j: (0, j)),
        out_shape=jax.ShapeDtypeStruct((B, n_out), F32),
        compiler_params=pltpu.CompilerParams(dimension_semantics=("arbitrary",),
                                             vmem_limit_bytes=VMEM_LIMIT),
    )(c, w_ada, b_ada.reshape(1, n_out))


def _rope(x, cos, sin_signed, first_half):
    n = x.shape[1]
    partner = jnp.where(first_half, pltpu.roll(x, n - 16, 1), pltpu.roll(x, 16, 1))
    return x * cos + partner * sin_signed


def _inproj_kernel(x_ref, mod_ref, gpre_ref, w_ref, wq_ref, qg_ref, kg_ref, cos_ref, sin_ref, ones_ref,
                   q_ref, k_ref, vt_ref, gatt_ref, rin_ref, grw_ref):
    D = x_ref.shape[2]
    x = x_ref[0]
    ms = jnp.mean(x * x, axis=-1, keepdims=True)
    xn = x * lax.rsqrt(ms + NORM_EPS) * gpre_ref[...]
    mod = mod_ref[0]
    shift = mod[:, :D]
    scale = mod[:, D:2 * D]
    h = (xn * (1.0 + scale) + shift).astype(BF16)

    def proj(lo, width):
        return jnp.dot(h, w_ref[:, lo:lo + width], preferred_element_type=F32)

    cos = cos_ref[...]
    sin = sin_ref[...]
    lane = lax.broadcasted_iota(jnp.int32, cos.shape, 1)
    first_half = (lane % 32) < 16

    def head_norm(y, g):
        w = y.shape[1]
        msq = _bdot(y * y, ones_ref[:w, :w]) * (1.0 / HEAD_DIM)
        return y * lax.rsqrt(msq + NORM_EPS) * g

    q = head_norm(jnp.dot(h, wq_ref[...], preferred_element_type=F32), qg_ref[...])
    reps = ATT_WIDTH // LANES
    cos_q = jnp.concatenate([cos] * reps, axis=1)
    sin_q = jnp.concatenate([sin] * reps, axis=1)
    lane_q = lax.broadcasted_iota(jnp.int32, cos_q.shape, 1)
    fh_q = (lane_q % 32) < 16
    q = _rope(q, cos_q, sin_q, fh_q) * Q_SCALE
    q_ref[0] = q.astype(BF16)

    k = head_norm(proj(K_OFF, ATT_KV_WIDTH), kg_ref[...])
    k_ref[0] = _rope(k, cos, sin, first_half).astype(BF16)

    v = proj(V_OFF, ATT_KV_WIDTH).astype(BF16)
    eye = (lax.broadcasted_iota(jnp.int32, (ATT_KV_WIDTH, ATT_KV_WIDTH), 0)
           == lax.broadcasted_iota(jnp.int32, (ATT_KV_WIDTH, ATT_KV_WIDTH), 1)).astype(BF16)
    vt = _bdot_nt(eye, v).astype(BF16)
    ones_rows = jnp.ones((VT_ONES, vt.shape[1]), BF16)
    vt_ref[0] = jnp.concatenate([vt[:HEAD_DIM], ones_rows, vt[HEAD_DIM:], ones_rows], axis=0)
    gatt_ref[0] = proj(GATT_OFF, ATT_WIDTH).astype(BF16)
    rin_ref[0] = proj(RIN_OFF, RWKV_SHIFT_WIDTH)
    grw_ref[0] = proj(GRW_OFF, RWKV_WIDTH).astype(BF16)


def _rope_tables(T):
    rows = T // GRID_W
    row = jnp.repeat(jnp.arange(rows, dtype=F32), GRID_W)
    col = jnp.tile(jnp.arange(GRID_W, dtype=F32), rows)
    n_freq = HEAD_DIM // 4
    inv_freq = ROPE_THETA ** (-jnp.arange(n_freq, dtype=F32) / n_freq)
    ang_r = row[:, None] * inv_freq
    ang_c = col[:, None] * inv_freq
    cr, sr, cc, sc = jnp.cos(ang_r), jnp.sin(ang_r), jnp.cos(ang_c), jnp.sin(ang_c)
    cos = jnp.concatenate([cr, cr, cc, cc], axis=1)
    sin = jnp.concatenate([-sr, sr, -sc, sc], axis=1)
    reps = LANES // HEAD_DIM
    return jnp.tile(cos, (1, reps)), jnp.tile(sin, (1, reps))


def _block_ones(n):
    i = jnp.arange(n) // HEAD_DIM
    return (i[:, None] == i[None, :]).astype(BF16)


def _inproj_call(x, mod3, g_pre, w_in_bf, wq_bf, qg, kg, cos, sin, ones, tm):
    B, T, D = x.shape
    grid = (T // tm, B)
    row = lambda t, b: (b, t, 0)
    const2 = lambda t, b: (0, 0)
    out_shapes = (
        jax.ShapeDtypeStruct((B, T, ATT_WIDTH), BF16),
        jax.ShapeDtypeStruct((B, T, ATT_KV_WIDTH), BF16),
        jax.ShapeDtypeStruct((B, VT_ROWS, T), BF16),
        jax.ShapeDtypeStruct((B, T, ATT_WIDTH), BF16),
        jax.ShapeDtypeStruct((B, T, RWKV_SHIFT_WIDTH), F32),
        jax.ShapeDtypeStruct((B, T, RWKV_WIDTH), BF16),
    )
    out_specs = tuple(
        pl.BlockSpec((1, VT_ROWS, tm), lambda t, b: (b, 0, t)) if i == 2
        else pl.BlockSpec((1, tm, s.shape[2]), row) for i, s in enumerate(out_shapes))
    return pl.pallas_call(
        _inproj_kernel,
        name="in_proj",
        grid=grid,
        in_specs=[pl.BlockSpec((1, tm, D), row),
                  pl.BlockSpec((1, 1, mod3.shape[2]), lambda t, b: (b, 0, 0)),
                  pl.BlockSpec((1, D), const2),
                  pl.BlockSpec((D, IN_WIDTH), const2),
                  pl.BlockSpec((D, ATT_WIDTH), const2),
                  pl.BlockSpec((1, ATT_WIDTH), const2),
                  pl.BlockSpec((1, ATT_KV_WIDTH), const2),
                  pl.BlockSpec((tm, LANES), lambda t, b: (t, 0)),
                  pl.BlockSpec((tm, LANES), lambda t, b: (t, 0)),
                  pl.BlockSpec((ATT_WIDTH, ATT_WIDTH), const2)],
        out_specs=out_specs,
        out_shape=out_shapes,
        compiler_params=pltpu.CompilerParams(dimension_semantics=("arbitrary", "arbitrary"),
                                             vmem_limit_bytes=VMEM_LIMIT),
    )(x, mod3, g_pre, w_in_bf, wq_bf, qg, kg, cos, sin, ones)


def _attn_kernel(q_ref, k_ref, vt_ref, g_ref, o_ref):
    q = q_ref[0]
    k = k_ref[0]
    tq = q.shape[0]
    lane_lo = lax.broadcasted_iota(jnp.int32, (tq, PAIR), 1) < HEAD_DIM
    zero = jnp.zeros((tq, PAIR), BF16)
    def scores(j):
        qp = q[:, j * PAIR:(j + 1) * PAIR]
        qs = jnp.concatenate([jnp.where(lane_lo, qp, zero), jnp.where(lane_lo, zero, qp)], axis=0)
        return lax.dot_general(k, qs, (((1,), (1,)), ((), ())), preferred_element_type=F32)

    outs = [None] * ATT_Q_HEADS
    s_next = scores(0)
    for j in range(ATT_GROUP):
        s = s_next
        if j + 1 < ATT_GROUP:
            s_next = scores(j + 1)
        m = jnp.max(s, axis=0, keepdims=True)
        p = jnp.exp2(s - m).astype(BF16)
        for kv in range(ATT_KV_HEADS):
            vt = vt_ref[0, kv * VT_KV:(kv + 1) * VT_KV, :]
            o = jnp.dot(vt, p[:, kv * tq:(kv + 1) * tq], preferred_element_type=F32)
            outs[Q_HEAD_ORDER[j * ATT_KV_HEADS + kv]] = o[:HEAD_DIM] / o[HEAD_DIM:HEAD_DIM + 1]
    y = jnp.concatenate(outs, axis=0).T
    g = g_ref[0].astype(F32)
    o_ref[0] = (y * (g * _sigmoid(g))).astype(BF16)


def _attn_call(q, k, vt, g_att, tq):
    B, T, _ = q.shape
    return pl.pallas_call(
        _attn_kernel,
        name="gqa_attn",
        grid=(B, T // tq),
        in_specs=[pl.BlockSpec((1, tq, ATT_WIDTH), lambda b, i: (b, i, 0)),
                  pl.BlockSpec((1, T, ATT_KV_WIDTH), lambda b, i: (b, 0, 0)),
                  pl.BlockSpec((1, VT_ROWS, T), lambda b, i: (b, 0, 0)),
                  pl.BlockSpec((1, tq, ATT_WIDTH), lambda b, i: (b, i, 0))],
        out_specs=pl.BlockSpec((1, tq, ATT_WIDTH), lambda b, i: (b, i, 0)),
        out_shape=jax.ShapeDtypeStruct((B, T, ATT_WIDTH), BF16),
        compiler_params=pltpu.CompilerParams(dimension_semantics=("arbitrary", "arbitrary"),
                                             vmem_limit_bytes=VMEM_LIMIT),
    )(q, k, vt, g_att)


CHUNKS_PER_STEP = 2
N_PAIRS = RWKV_WIDTH // PAIR


def _later_rows(x, s, odd):
    first = s if odd else 0
    return jnp.concatenate([x[b:b + s] for b in range(first, x.shape[0], 2 * s)], axis=0)


def _merge_rows(keep, new, s, odd):
    out = []
    for i, b in enumerate(range(0, keep.shape[0], 2 * s)):
        lo, hi = keep[b:b + s], keep[b + s:b + 2 * s]
        blk = new[i * s:(i + 1) * s]
        out += [lo, blk] if odd else [blk, hi]
    return jnp.concatenate(out, axis=0)


INVERSE_STAGES = 9


def _inverse_stages(a_list, eye, blks, odds, out):
    blk8, blk16, blk32, blk64 = blks
    a8 = [jnp.where(blk8, a, 0.0) for a in a_list]
    a2 = [_bdot(x, x) for x in a8]
    yield
    a4 = [_bdot(x, x) for x in a2]
    t = [_bdot(eye - x, eye + y) for x, y in zip(a8, a2)]
    yield
    t = [_bdot(x, eye + y) for x, y in zip(t, a4)]
    for inner, outer, s in ((blk8, blk16, 8), (blk16, blk32, 16), (blk32, blk64, 32)):
        yield
        off = jnp.logical_and(outer, jnp.logical_not(inner))
        xs = [_bdot(_later_rows(x, s, o), jnp.where(off, a, 0.0)) for x, a, o in zip(t, a_list, odds)]
        yield
        upd = [_later_rows(x, s, o) - _bdot(y, x) for x, y, o in zip(t, xs, odds)]
        t = [_merge_rows(x, u, s, o) for x, u, o in zip(t, upd, odds)]
    out.extend(t)


def _rwkv_kernel(fm_ref, fp_ref, fn_ref, bm_ref, bp_ref, bn_ref, taps_ref, tri_ref,
                 wup_ref, w0_ref, aup_ref, a0_ref, kk_ref, ka_ref, rk_ref, ones_ref,
                 yf_ref, yb_ref, bon_ref,
                 h_scr, x1_scr, x2_scr, x3_scr, v_scr, gam_scr):
    s = pl.program_id(1)

    @pl.when(s == 0)
    def _():
        def zero_chain(i, carry):
            for ref in (x1_scr, x2_scr, x3_scr, v_scr, gam_scr):
                ref[i] = jnp.zeros(ref.shape[1:], ref.dtype)
            return carry

        def zero_state(i, carry):
            h_scr[i] = jnp.zeros(h_scr.shape[1:], h_scr.dtype)
            return carry

        lax.fori_loop(0, x1_scr.shape[0], zero_chain, 0)
        lax.fori_loop(0, h_scr.shape[0], zero_state, 0)

    n_blocks = pl.num_programs(1) - 1
    SL = fm_ref.shape[1]
    L = CHUNK
    cps = SL // L
    W = RWKV_WIDTH
    n = 2 * L
    n_chains = x1_scr.shape[0]

    ri = lax.broadcasted_iota(jnp.int32, (n, n), 0)
    ci = lax.broadcasted_iota(jnp.int32, (n, n), 1)
    eye = ri == ci
    blk = lambda size: (ri // size) == (ci // size)
    blk8, blk16, blk32, blk64 = blk(8), blk(16), blk(32), blk(64)
    stricts = (jnp.logical_and(blk64, ci < ri), jnp.logical_and(blk64, ci > ri))
    chains = [(z, j, p) for z in range(N_DIRS) for j in range(cps) for p in range(N_PAIRS)]
    assert len(chains) == n_chains
    strict = [stricts[z] for z, _, _ in chains]
    incl = [jnp.logical_or(m, eye) for m in strict]
    odds = [z == 0 for z, _, _ in chains]
    blks = (blk8, blk16, blk32, blk64)

    x1 = [x1_scr[i] for i in range(n_chains)]
    x2 = [x2_scr[i] for i in range(n_chains)]
    x3 = [x3_scr[i] for i in range(n_chains)]
    v_s = [v_scr[i] for i in range(n_chains)]
    gam_rows = [gam_scr[i][0:1] for i in range(n_chains)]
    gram = [_bdot_nt(a, b) for a, b in zip(x1, x2)]
    akk_m = [jnp.where(m, g[:n, :n], 0.0) for m, g in zip(strict, gram)]
    akb_m = [jnp.where(m, g[:n, n:], 0.0) for m, g in zip(strict, gram)]
    ark_m = [jnp.where(m, g[n:, :n], 0.0) for m, g in zip(incl, gram)]
    arb_m = [jnp.where(m, g[n:, n:], 0.0) for m, g in zip(incl, gram)]

    t_inv = []
    inverse = _inverse_stages(akb_m, eye, blks, odds, t_inv)
    av = [None] * n_chains

    def apply_values(lo, hi):
        for i in range(lo, hi):
            av[i] = _bdot(jnp.concatenate([akk_m[i], ark_m[i]], axis=0), v_s[i])

    next(inverse, None)
    next(inverse, None)

    taps = taps_ref[...]
    row8 = lax.broadcasted_iota(jnp.int32, (SUBLANES, 1), 0)

    def shifted(m_ref, p_ref, n_ref, block):
        main = m_ref[0]
        prev_row = jnp.where(block > 0, p_ref[0][SUBLANES - 1:SUBLANES], 0.0)
        next_row = jnp.where(block < n_blocks - 1, n_ref[0][0:1], 0.0)
        up = pltpu.roll(main, 1, 0)
        dn = pltpu.roll(main, SL - 1, 0)
        up = jnp.concatenate([jnp.where(row8 == 0, prev_row, up[:SUBLANES]), up[SUBLANES:]], axis=0)
        dn = jnp.concatenate([dn[:SL - SUBLANES],
                              jnp.where(row8 == SUBLANES - 1, next_row, dn[SL - SUBLANES:])], axis=0)
        return taps[0:1] * up + taps[1:2] * main + taps[2:3] * dn

    cf = jnp.minimum(s, n_blocks - 1)
    cb = jnp.maximum(n_blocks - 1 - s, 0)
    sh = (shifted(fm_ref, fp_ref, fn_ref, cf), shifted(bm_ref, bp_ref, bn_ref, cb))
    r = [x[:, 0:W] for x in sh]
    k = [x[:, W:2 * W] for x in sh]
    v = [x[:, 2 * W:3 * W] for x in sh]
    wd = [x[:, 3 * W:3 * W + LORA_RANK] for x in sh]
    ad = [x[:, 3 * W + LORA_RANK:3 * W + 2 * LORA_RANK] for x in sh]
    lora_w = [_bdot(jnp.tanh(wd[z]), wup_ref[z]) for z in range(N_DIRS)]
    lora_a = [_bdot(ad[z], aup_ref[z]) for z in range(N_DIRS)]
    lora_a0b = _bdot(ad[1], aup_ref[0])
    kkr = [x * kk_ref[...] for x in k]
    kk_ss = [_bdot(x * x, ones_ref[...]) for x in kkr]

    next(inverse, None)
    next(inverse, None)

    lw = [-DECAY_SCALE * _sigmoid(w0_ref[z] + lora_w[z]) for z in range(N_DIRS)]
    a = [_sigmoid(a0_ref[z] + lora_a[z]) for z in range(N_DIRS)]
    kk = [x * lax.rsqrt(y + L2_EPS) for x, y in zip(kkr, kk_ss)]
    kt = [k[z] * (1.0 + (a[z] - 1.0) * ka_ref[...]) for z in range(N_DIRS)]
    akk = [a[z] * kk[z] for z in range(N_DIRS)]
    lw_hi = [x.astype(BF16) for x in lw]
    lw_lo = [(x - y.astype(F32)).astype(BF16) for x, y in zip(lw, lw_hi)]
    g = [jnp.dot(tri_ref[z], lw_hi[z], preferred_element_type=F32)
         + jnp.dot(tri_ref[z], lw_lo[z], preferred_element_type=F32) for z in range(N_DIRS)]
    a0_b = _sigmoid(a0_ref[0] + lora_a0b)
    kt0_b = k[1] * (1.0 + (a0_b - 1.0) * ka_ref[...])
    rk_sum = r[1] * (kt0_b + kt[1]) * rk_ref[...]
    rk_hi = rk_sum.astype(BF16)
    rk_lo = (rk_sum - rk_hi.astype(F32)).astype(BF16)
    bonus_dots = (jnp.dot(rk_hi, ones_ref[...], preferred_element_type=F32)
                  + jnp.dot(rk_lo, ones_ref[...], preferred_element_type=F32))

    quarter = n_chains // 4
    for part in range(4):
        next(inverse, None)
        apply_values(part * quarter, (part + 1) * quarter)
    next(inverse, None)
    assert len(t_inv) == n_chains
    wu = [_bdot(t, jnp.concatenate([p[:n], q[:n].astype(BF16)], axis=1))
          for t, p, q in zip(t_inv, x1, av)]
    aw = [_bdot(p, q) for p, q in zip(arb_m, wu)]
    qe = [p[n:].astype(F32) - q[:, :PAIR] for p, q in zip(x1, aw)]
    yv = [p[n:] - q[:, PAIR:] for p, q in zip(av, aw)]
    c0 = [_bdot_tn(p, jnp.concatenate([q, (-u[:, PAIR:]).astype(BF16)], axis=0))
          for p, q, u in zip(x3, v_s, wu)]
    bw = [_bdot_tn(p[n:], u[:, :PAIR]) for p, u in zip(x3, wu)]
    m_c = [jnp.where(eye, jnp.broadcast_to(g, (n, PAIR)), 0.0) - q for g, q in zip(gam_rows, bw)]

    lanes = [(z, p) for z in range(N_DIRS) for p in range(N_PAIRS)]
    h = [h_scr[z * N_PAIRS + p] for z, p in lanes]
    ys = [None] * n_chains
    for step in range(cps):
        idx = [chains.index((z, step if z == 0 else cps - 1 - step, p)) for z, p in lanes]
        y_s = [_bdot(qe[i], hh) + yv[i] for i, hh in zip(idx, h)]
        h = [_bdot(m_c[i], hh) + c0[i] for i, hh in zip(idx, h)]
        for i, y in zip(idx, y_s):
            ys[i] = y[:L] + y[L:]
    for (z, p), hh in zip(lanes, h):
        h_scr[z * N_PAIRS + p] = hh

    def assemble(z):
        return jnp.concatenate(
            [jnp.concatenate([ys[chains.index((z, j, p))] for p in range(N_PAIRS)], axis=1)
             for j in range(cps)], axis=0)

    yf_ref[0] = assemble(0).astype(BF16)
    yb_ref[0] = assemble(1).astype(BF16)

    bon_ref[0] = (bonus_dots * v[1]).astype(BF16)
    lane_lo = lax.broadcasted_iota(jnp.int32, (L, PAIR), 1) < HEAD_DIM
    zero = jnp.zeros((L, PAIR), BF16)

    def stack(x, j, p):
        xp = x[j * L:(j + 1) * L, p * PAIR:(p + 1) * PAIR]
        return jnp.concatenate([jnp.where(lane_lo, xp, zero), jnp.where(lane_lo, zero, xp)], axis=0)

    for z in range(N_DIRS):
        g_tot = [jnp.sum(lw[z][j * L:(j + 1) * L], axis=0, keepdims=True) for j in range(cps)]
        g_tot_rows = jnp.concatenate([jnp.broadcast_to(x, (L, W)) for x in g_tot], axis=0)
        e_out = jnp.exp(-g[z])
        e_rem = jnp.exp(g_tot_rows - g[z])
        kkg = (kk[z] * jnp.exp(g[z] - lw[z])).astype(BF16)
        rg = (r[z] * jnp.exp(g[z])).astype(BF16)
        kd = (kt[z] * e_out).astype(BF16)
        bd = (akk[z] * e_out).astype(BF16)
        kdg = (kt[z] * e_rem).astype(BF16)
        bdg = (akk[z] * e_rem).astype(BF16)
        v_bf = v[z].astype(BF16)
        for j in range(cps):
            gam = jnp.broadcast_to(jnp.exp(g_tot[j]), (SUBLANES, W))
            for p in range(N_PAIRS):
                i = chains.index((z, j, p))
                x1_scr[i] = jnp.concatenate([stack(kkg, j, p), stack(rg, j, p)], axis=0)
                x2_scr[i] = jnp.concatenate([stack(kd, j, p), stack(bd, j, p)], axis=0)
                x3_scr[i] = jnp.concatenate([stack(kdg, j, p), stack(bdg, j, p)], axis=0)
                v_scr[i] = stack(v_bf, j, p)
                gam_scr[i] = gam[:, p * PAIR:(p + 1) * PAIR]


def _rwkv_call(rin, taps, tri, w_up, w0, a_up, a0, k_k, k_a, r_k, ones):
    B, T, C = rin.shape
    L = CHUNK * CHUNKS_PER_STEP
    nC = T // L
    hb = L // SUBLANES
    n_hb = T // SUBLANES
    W = RWKV_WIDTH
    n_chains = N_DIRS * CHUNKS_PER_STEP * N_PAIRS

    def prep_f(s): return jnp.minimum(s, nC - 1)
    def prep_b(s): return jnp.maximum(nC - 1 - s, 0)
    def main_f(b, s): return (b, prep_f(s), 0)
    def prev_f(b, s): return (b, jnp.maximum(prep_f(s) * hb - 1, 0), 0)
    def next_f(b, s): return (b, jnp.minimum((prep_f(s) + 1) * hb, n_hb - 1), 0)
    def main_b(b, s): return (b, prep_b(s), 0)
    def prev_b(b, s): return (b, jnp.maximum(prep_b(s) * hb - 1, 0), 0)
    def next_b(b, s): return (b, jnp.minimum((prep_b(s) + 1) * hb, n_hb - 1), 0)
    def out_f(b, s): return (b, jnp.maximum(s - 1, 0), 0)
    def out_b(b, s): return (b, jnp.minimum(nC - s, nC - 1), 0)

    const2 = lambda b, s: (0, 0)
    const3 = lambda b, s: (0, 0, 0)
    out_shape = tuple(jax.ShapeDtypeStruct((B, T, W), BF16) for _ in range(3))
    n = 2 * CHUNK
    return pl.pallas_call(
        _rwkv_kernel,
        name="rwkv7_chunked",
        grid=(B, nC + 1),
        in_specs=[pl.BlockSpec((1, L, C), main_f),
                  pl.BlockSpec((1, SUBLANES, C), prev_f),
                  pl.BlockSpec((1, SUBLANES, C), next_f),
                  pl.BlockSpec((1, L, C), main_b),
                  pl.BlockSpec((1, SUBLANES, C), prev_b),
                  pl.BlockSpec((1, SUBLANES, C), next_b),
                  pl.BlockSpec((3, C), const2),
                  pl.BlockSpec((N_DIRS, L, L), const3),
                  pl.BlockSpec((N_DIRS, LORA_RANK, W), const3),
                  pl.BlockSpec((N_DIRS, 1, W), const3),
                  pl.BlockSpec((N_DIRS, LORA_RANK, W), const3),
                  pl.BlockSpec((N_DIRS, 1, W), const3),
                  pl.BlockSpec((1, W), const2),
                  pl.BlockSpec((1, W), const2),
                  pl.BlockSpec((1, W), const2),
                  pl.BlockSpec((W, W), const2)],
        out_specs=(pl.BlockSpec((1, L, W), out_f),
                   pl.BlockSpec((1, L, W), out_b),
                   pl.BlockSpec((1, L, W), main_b)),
        out_shape=out_shape,
        scratch_shapes=[pltpu.VMEM((N_DIRS * N_PAIRS, n, PAIR), F32),
                        pltpu.VMEM((n_chains, 2 * n, PAIR), BF16),
                        pltpu.VMEM((n_chains, 2 * n, PAIR), BF16),
                        pltpu.VMEM((n_chains, 2 * n, PAIR), BF16),
                        pltpu.VMEM((n_chains, n, PAIR), BF16),
                        pltpu.VMEM((n_chains, SUBLANES, PAIR), F32)],
        compiler_params=pltpu.CompilerParams(dimension_semantics=("arbitrary", "arbitrary"),
                                             vmem_limit_bytes=VMEM_LIMIT),
    )(rin, rin, rin, rin, rin, rin, taps, tri, w_up, w0, a_up, a0, k_k, k_a, r_k, ones)


def _outproj_kernel(att_ref, yf_ref, yb_ref, bon_ref, grw_ref, x_ref, mod_ref, w_ref,
                    gnw_ref, gnb_ref, gpost_ref, ones_ref, o_ref):
    y = yf_ref[0].astype(F32) + yb_ref[0].astype(F32)
    ones = ones_ref[...]
    inv_n = 1.0 / HEAD_DIM
    mu = _bdot(y, ones) * inv_n
    d = y - mu
    var = _bdot(d * d, ones) * inv_n
    yn = d * lax.rsqrt(var + GN_EPS) * gnw_ref[...] + gnb_ref[...]
    g = grw_ref[0].astype(F32)
    rw = ((yn + bon_ref[0].astype(F32)) * (g * _sigmoid(g))).astype(BF16)
    out = (jnp.dot(att_ref[0], w_ref[:ATT_WIDTH, :], preferred_element_type=F32)
           + jnp.dot(rw, w_ref[ATT_WIDTH:, :], preferred_element_type=F32))
    ms = jnp.mean(out * out, axis=-1, keepdims=True)
    on = out * lax.rsqrt(ms + NORM_EPS) * gpost_ref[...]
    o_ref[0] = x_ref[0] + mod_ref[0] * on


def _outproj_call(att, yf, yb, bon, grw, x, mod3, w_out_bf, gn_w, gn_b, g_post, ones, tm):
    B, T, D = x.shape
    W = RWKV_WIDTH
    row = lambda t, b: (b, t, 0)
    const2 = lambda t, b: (0, 0)
    return pl.pallas_call(
        _outproj_kernel,
        name="out_proj",
        grid=(T // tm, B),
        in_specs=[pl.BlockSpec((1, tm, ATT_WIDTH), row),
                  pl.BlockSpec((1, tm, W), row),
                  pl.BlockSpec((1, tm, W), row),
                  pl.BlockSpec((1, tm, W), row),
                  pl.BlockSpec((1, tm, W), row),
                  pl.BlockSpec((1, tm, D), row),
                  pl.BlockSpec((1, 1, D), lambda t, b: (b, 0, 2)),
                  pl.BlockSpec((ATT_WIDTH + W, D), const2),
                  pl.BlockSpec((1, W), const2),
                  pl.BlockSpec((1, W), const2),
                  pl.BlockSpec((1, D), const2),
                  pl.BlockSpec((W, W), const2)],
        out_specs=pl.BlockSpec((1, tm, D), row),
        out_shape=jax.ShapeDtypeStruct((B, T, D), F32),
        compiler_params=pltpu.CompilerParams(dimension_semantics=("arbitrary", "arbitrary"),
                                             vmem_limit_bytes=VMEM_LIMIT),
    )(att, yf, yb, bon, grw, x, mod3, w_out_bf, gn_w, gn_b, g_post, ones)


def _pick_tile(T, target):
    t = min(T, target)
    while T % t:
        t //= 2
    return t


def _permute_heads(w, axis):
    blocks = jnp.split(w, ATT_Q_HEADS, axis=axis)
    return jnp.concatenate([blocks[h] for h in Q_HEAD_ORDER], axis=axis)


def kernel(x, c, w_ada, b_ada, g_pre, w_in, q_norm_g, k_norm_g, shift_taps, w_up, w0, a_up, a0,
           k_k, k_a, r_k, gn_w, gn_b, w_out, g_post):
    B, T, D = x.shape
    depth = w_ada.shape[0]
    assert T % (CHUNK * CHUNKS_PER_STEP) == 0 and T % GRID_W == 0
    tm_in = _pick_tile(T, IN_PROJ_ROWS)
    tm_out = _pick_tile(T, OUT_PROJ_ROWS)
    tq = _pick_tile(T, ATTN_QUERY_ROWS)
    cos, sin = _rope_tables(T)
    ones = _block_ones(RWKV_WIDTH)
    ti = jnp.arange(CHUNK * CHUNKS_PER_STEP)
    same_chunk = (ti[None, :] // CHUNK) == (ti[:, None] // CHUNK)
    tri = jnp.stack([same_chunk & (ti[None, :] <= ti[:, None]),
                     same_chunk & (ti[None, :] >= ti[:, None])]).astype(BF16)
    for l in range(depth):
        mod = _ada_call(c, w_ada[l], b_ada[l])
        mod3 = mod.reshape(B, 1, 3 * D)
        qg = jnp.tile(q_norm_g[l], ATT_Q_HEADS).reshape(1, ATT_WIDTH)
        kg = jnp.tile(k_norm_g[l], ATT_KV_HEADS).reshape(1, ATT_KV_WIDTH)
        wq = _permute_heads(w_in[l][:, Q_OFF:K_OFF], 1).astype(BF16)
        q, k, vt, g_att, rin, g_rw = _inproj_call(
            x, mod3, g_pre[l].reshape(1, D), w_in[l].astype(BF16), wq, qg, kg, cos, sin, ones, tm_in)
        att = _attn_call(q, k, vt, g_att, tq)
        yf, yb, bon = _rwkv_call(
            rin, shift_taps[l], tri, w_up[l].astype(BF16), w0[l].reshape(N_DIRS, 1, RWKV_WIDTH),
            a_up[l].astype(BF16), a0[l].reshape(N_DIRS, 1, RWKV_WIDTH),
            k_k[l].reshape(1, RWKV_WIDTH), k_a[l].reshape(1, RWKV_WIDTH),
            r_k[l].reshape(1, RWKV_WIDTH), ones)
        x = _outproj_call(att, yf, yb, bon, g_rw, x, mod3, w_out[l].astype(BF16),
                          gn_w[l].reshape(1, RWKV_WIDTH), gn_b[l].reshape(1, RWKV_WIDTH),
                          g_post[l].reshape(1, D), ones, tm_out)
    return x
```

```python
import functools

import jax
import jax.numpy as jnp
from jax import lax
from jax.experimental import pallas as pl
from jax.experimental.pallas import tpu as pltpu

F32 = jnp.float32
BF16 = jnp.bfloat16

HEAD_DIM = 64
ATT_Q_HEADS = 8
ATT_KV_HEADS = 2
ATT_GROUP = ATT_Q_HEADS // ATT_KV_HEADS
ATT_WIDTH = ATT_Q_HEADS * HEAD_DIM
ATT_KV_WIDTH = ATT_KV_HEADS * HEAD_DIM
RWKV_HEADS = 8
RWKV_WIDTH = RWKV_HEADS * HEAD_DIM
LORA_RANK = 64
RWKV_SHIFT_WIDTH = 3 * RWKV_WIDTH + 2 * LORA_RANK
N_DIRS = 2
GRID_W = 64
ROPE_THETA = 10000.0
DECAY_SCALE = 0.6065306597126334
NORM_EPS = 1e-6
GN_EPS = 64e-5
L2_EPS = 1e-12

Q_OFF = 0
K_OFF = Q_OFF + ATT_WIDTH
V_OFF = K_OFF + ATT_KV_WIDTH
GATT_OFF = V_OFF + ATT_KV_WIDTH
RIN_OFF = GATT_OFF + ATT_WIDTH
GRW_OFF = RIN_OFF + RWKV_SHIFT_WIDTH
IN_WIDTH = GRW_OFF + RWKV_WIDTH

LANES = 128
SUBLANES = 8
CHUNK = 64
PAIR = 2 * HEAD_DIM
BF16_SUBLANES = 16
VT_ONES = BF16_SUBLANES
VT_KV = HEAD_DIM + VT_ONES
VT_ROWS = ATT_KV_HEADS * VT_KV
Q_SCALE = HEAD_DIM ** -0.5 * 1.4426950408889634
Q_HEAD_ORDER = tuple(g + kv * ATT_GROUP for g in range(ATT_GROUP) for kv in range(ATT_KV_HEADS))
VMEM_LIMIT = 56 * 1024 * 1024
IN_PROJ_ROWS = 256
OUT_PROJ_ROWS = 512
ATTN_QUERY_ROWS = 512


def _bdot(a, b):
    return jnp.dot(a.astype(BF16), b.astype(BF16), preferred_element_type=F32)


def _bdot_nt(a, b):
    return lax.dot_general(a.astype(BF16), b.astype(BF16), (((1,), (1,)), ((), ())),
                           preferred_element_type=F32)


def _bdot_tn(a, b):
    return lax.dot_general(a.astype(BF16), b.astype(BF16), (((0,), (0,)), ((), ())),
                           preferred_element_type=F32)


def _sigmoid(x):
    return 1.0 / (1.0 + jnp.exp(-x))


def _ada_kernel(c_ref, w_ref, b_ref, o_ref):
    c = c_ref[...]
    ca = c * _sigmoid(c)
    o_ref[...] = jnp.dot(ca, w_ref[...], precision=lax.Precision.HIGHEST,
                         preferred_element_type=F32) + b_ref[...]


def _ada_call(c, w_ada, b_ada):
    B, D = c.shape
    n_out = w_ada.shape[1]
    tn = D
    return pl.pallas_call(
        _ada_kernel,
        name="ada_mod",
        grid=(n_out // tn,),
        in_specs=[pl.BlockSpec((B, D), lambda j: (0, 0)),
                  pl.BlockSpec((D, tn), lambda j: (0, j)),
                  pl.BlockSpec((1, tn), lambda j: (0, j))],
        out_specs=pl.BlockSpec((B, tn), lambda j: (0, j)),
        out_shape=jax.ShapeDtypeStruct((B, n_out), F32),
        compiler_params=pltpu.CompilerParams(dimension_semantics=("arbitrary",),
                                             vmem_limit_bytes=VMEM_LIMIT),
    )(c, w_ada, b_ada.reshape(1, n_out))


def _rope(x, cos, sin_signed, first_half):
    n = x.shape[1]
    partner = jnp.where(first_half, pltpu.roll(x, n - 16, 1), pltpu.roll(x, 16, 1))
    return x * cos + partner * sin_signed


def _inproj_kernel(x_ref, mod_ref, gpre_ref, w_ref, wq_ref, qg_ref, kg_ref, cos_ref, sin_ref, ones_ref,
                   q_ref, k_ref, vt_ref, gatt_ref, rin_ref, grw_ref):
    D = x_ref.shape[2]
    x = x_ref[0]
    ms = jnp.mean(x * x, axis=-1, keepdims=True)
    xn = x * lax.rsqrt(ms + NORM_EPS) * gpre_ref[...]
    mod = mod_ref[0]
    shift = mod[:, :D]
    scale = mod[:, D:2 * D]
    h = (xn * (1.0 + scale) + shift).astype(BF16)

    def proj(lo, width):
        return jnp.dot(h, w_ref[:, lo:lo + width], preferred_element_type=F32)

    cos = cos_ref[...]
    sin = sin_ref[...]
    lane = lax.broadcasted_iota(jnp.int32, cos.shape, 1)
    first_half = (lane % 32) < 16

    def head_norm(y, g):
        w = y.shape[1]
        msq = _bdot(y * y, ones_ref[:w, :w]) * (1.0 / HEAD_DIM)
        return y * lax.rsqrt(msq + NORM_EPS) * g

    q = head_norm(jnp.dot(h, wq_ref[...], preferred_element_type=F32), qg_ref[...])
    reps = ATT_WIDTH // LANES
    cos_q = jnp.concatenate([cos] * reps, axis=1)
    sin_q = jnp.concatenate([sin] * reps, axis=1)
    lane_q = lax.broadcasted_iota(jnp.int32, cos_q.shape, 1)
    fh_q = (lane_q % 32) < 16
    q = _rope(q, cos_q, sin_q, fh_q) * Q_SCALE
    q_ref[0] = q.astype(BF16)

    k = head_norm(proj(K_OFF, ATT_KV_WIDTH), kg_ref[...])
    k_ref[0] = _rope(k, cos, sin, first_half).astype(BF16)

    v = proj(V_OFF, ATT_KV_WIDTH).astype(BF16)
    eye = (lax.broadcasted_iota(jnp.int32, (ATT_KV_WIDTH, ATT_KV_WIDTH), 0)
           == lax.broadcasted_iota(jnp.int32, (ATT_KV_WIDTH, ATT_KV_WIDTH), 1)).astype(BF16)
    vt = _bdot_nt(eye, v).astype(BF16)
    ones_rows = jnp.ones((VT_ONES, vt.shape[1]), BF16)
    vt_ref[0] = jnp.concatenate([vt[:HEAD_DIM], ones_rows, vt[HEAD_DIM:], ones_rows], axis=0)
    gatt_ref[0] = proj(GATT_OFF, ATT_WIDTH).astype(BF16)
    rin_ref[0] = proj(RIN_OFF, RWKV_SHIFT_WIDTH)
    grw_ref[0] = proj(GRW_OFF, RWKV_WIDTH).astype(BF16)


def _rope_tables(T):
    rows = T // GRID_W
    row = jnp.repeat(jnp.arange(rows, dtype=F32), GRID_W)
    col = jnp.tile(jnp.arange(GRID_W, dtype=F32), rows)
    n_freq = HEAD_DIM // 4
    inv_freq = ROPE_THETA ** (-jnp.arange(n_freq, dtype=F32) / n_freq)
    ang_r = row[:, None] * inv_freq
    ang_c = col[:, None] * inv_freq
    cr, sr, cc, sc = jnp.cos(ang_r), jnp.sin(ang_r), jnp.cos(ang_c), jnp.sin(ang_c)
    cos = jnp.concatenate([cr, cr, cc, cc], axis=1)
    sin = jnp.concatenate([-sr, sr, -sc, sc], axis=1)
    reps = LANES // HEAD_DIM
    return jnp.tile(cos, (1, reps)), jnp.tile(sin, (1, reps))


def _block_ones(n):
    i = jnp.arange(n) // HEAD_DIM
    return (i[:, None] == i[None, :]).astype(BF16)


def _inproj_call(x, mod3, g_pre, w_in_bf, wq_bf, qg, kg, cos, sin, ones, tm):
    B, T, D = x.shape
    grid = (T // tm, B)
    row = lambda t, b: (b, t, 0)
    const2 = lambda t, b: (0, 0)
    out_shapes = (
        jax.ShapeDtypeStruct((B, T, ATT_WIDTH), BF16),
        jax.ShapeDtypeStruct((B, T, ATT_KV_WIDTH), BF16),
        jax.ShapeDtypeStruct((B, VT_ROWS, T), BF16),
        jax.ShapeDtypeStruct((B, T, ATT_WIDTH), BF16),
        jax.ShapeDtypeStruct((B, T, RWKV_SHIFT_WIDTH), F32),
        jax.ShapeDtypeStruct((B, T, RWKV_WIDTH), BF16),
    )
    out_specs = tuple(
        pl.BlockSpec((1, VT_ROWS, tm), lambda t, b: (b, 0, t)) if i == 2
        else pl.BlockSpec((1, tm, s.shape[2]), row) for i, s in enumerate(out_shapes))
    return pl.pallas_call(
        _inproj_kernel,
        name="in_proj",
        grid=grid,
        in_specs=[pl.BlockSpec((1, tm, D), row),
                  pl.BlockSpec((1, 1, mod3.shape[2]), lambda t, b: (b, 0, 0)),
                  pl.BlockSpec((1, D), const2),
                  pl.BlockSpec((D, IN_WIDTH), const2),
                  pl.BlockSpec((D, ATT_WIDTH), const2),
                  pl.BlockSpec((1, ATT_WIDTH), const2),
                  pl.BlockSpec((1, ATT_KV_WIDTH), const2),
                  pl.BlockSpec((tm, LANES), lambda t, b: (t, 0)),
                  pl.BlockSpec((tm, LANES), lambda t, b: (t, 0)),
                  pl.BlockSpec((ATT_WIDTH, ATT_WIDTH), const2)],
        out_specs=out_specs,
        out_shape=out_shapes,
        compiler_params=pltpu.CompilerParams(dimension_semantics=("arbitrary", "arbitrary"),
                                             vmem_limit_bytes=VMEM_LIMIT),
    )(x, mod3, g_pre, w_in_bf, wq_bf, qg, kg, cos, sin, ones)


def _attn_kernel(q_ref, k_ref, vt_ref, g_ref, o_ref):
    q = q_ref[0]
    k = k_ref[0]
    tq = q.shape[0]
    lane_lo = lax.broadcasted_iota(jnp.int32, (tq, PAIR), 1) < HEAD_DIM
    zero = jnp.zeros((tq, PAIR), BF16)
    def scores(j):
        qp = q[:, j * PAIR:(j + 1) * PAIR]
        qs = jnp.concatenate([jnp.where(lane_lo, qp, zero), jnp.where(lane_lo, zero, qp)], axis=0)
        return lax.dot_general(k, qs, (((1,), (1,)), ((), ())), preferred_element_type=F32)

    outs = [None] * ATT_Q_HEADS
    s_next = scores(0)
    for j in range(ATT_GROUP):
        s = s_next
        if j + 1 < ATT_GROUP:
            s_next = scores(j + 1)
        m = jnp.max(s, axis=0, keepdims=True)
        p = jnp.exp2(s - m).astype(BF16)
        for kv in range(ATT_KV_HEADS):
            vt = vt_ref[0, kv * VT_KV:(kv + 1) * VT_KV, :]
            o = jnp.dot(vt, p[:, kv * tq:(kv + 1) * tq], preferred_element_type=F32)
            outs[Q_HEAD_ORDER[j * ATT_KV_HEADS + kv]] = o[:HEAD_DIM] / o[HEAD_DIM:HEAD_DIM + 1]
    y = jnp.concatenate(outs, axis=0).T
    g = g_ref[0].astype(F32)
    o_ref[0] = (y * (g * _sigmoid(g))).astype(BF16)


def _attn_call(q, k, vt, g_att, tq):
    B, T, _ = q.shape
    return pl.pallas_call(
        _attn_kernel,
        name="gqa_attn",
        grid=(B, T // tq),
        in_specs=[pl.BlockSpec((1, tq, ATT_WIDTH), lambda b, i: (b, i, 0)),
                  pl.BlockSpec((1, T, ATT_KV_WIDTH), lambda b, i: (b, 0, 0)),
                  pl.BlockSpec((1, VT_ROWS, T), lambda b, i: (b, 0, 0)),
                  pl.BlockSpec((1, tq, ATT_WIDTH), lambda b, i: (b, i, 0))],
        out_specs=pl.BlockSpec((1, tq, ATT_WIDTH), lambda b, i: (b, i, 0)),
        out_shape=jax.ShapeDtypeStruct((B, T, ATT_WIDTH), BF16),
        compiler_params=pltpu.CompilerParams(dimension_semantics=("arbitrary", "arbitrary"),
                                             vmem_limit_bytes=VMEM_LIMIT),
    )(q, k, vt, g_att)


CHUNKS_PER_STEP = 4
N_PAIRS = RWKV_WIDTH // PAIR


def _later_rows(x, s, odd):
    first = s if odd else 0
    return jnp.concatenate([x[b:b + s] for b in range(first, x.shape[0], 2 * s)], axis=0)


def _merge_rows(keep, new, s, odd):
    out = []
    for i, b in enumerate(range(0, keep.shape[0], 2 * s)):
        lo, hi = keep[b:b + s], keep[b + s:b + 2 * s]
        blk = new[i * s:(i + 1) * s]
        out += [lo, blk] if odd else [blk, hi]
    return jnp.concatenate(out, axis=0)


INVERSE_STAGES = 9


def _inverse_stages(a_list, eye, blks, odds, out):
    blk8, blk16, blk32, blk64 = blks
    a8 = [jnp.where(blk8, a, 0.0) for a in a_list]
    a2 = [_bdot(x, x) for x in a8]
    yield
    a4 = [_bdot(x, x) for x in a2]
    t = [_bdot(eye - x, eye + y) for x, y in zip(a8, a2)]
    yield
    t = [_bdot(x, eye + y) for x, y in zip(t, a4)]
    for inner, outer, s in ((blk8, blk16, 8), (blk16, blk32, 16), (blk32, blk64, 32)):
        yield
        off = jnp.logical_and(outer, jnp.logical_not(inner))
        xs = [_bdot(_later_rows(x, s, o), jnp.where(off, a, 0.0)) for x, a, o in zip(t, a_list, odds)]
        yield
        upd = [_later_rows(x, s, o) - _bdot(y, x) for x, y, o in zip(t, xs, odds)]
        t = [_merge_rows(x, u, s, o) for x, u, o in zip(t, upd, odds)]
    out.extend(t)


def _rwkv_kernel(fm_ref, fp_ref, fn_ref, bm_ref, bp_ref, bn_ref, taps_ref, tri_ref,
                 wup_ref, w0_ref, aup_ref, a0_ref, kk_ref, ka_ref, rk_ref, ones_ref,
                 yf_ref, yb_ref, bon_ref,
                 h_scr, x1_scr, x2_scr, x3_scr, v_scr, gam_scr, *, n_blocks):
    g_step = pl.program_id(0)
    last_block = pl.num_programs(0) - 2
    s = lax.rem(jnp.minimum(g_step, last_block), jnp.int32(n_blocks))
    chains_start_sequence = lax.rem(jnp.maximum(g_step - 1, 0), jnp.int32(n_blocks)) == 0

    @pl.when(g_step == 0)
    def _():
        def zero_chain(i, carry):
            for ref in (x1_scr, x2_scr, x3_scr, v_scr, gam_scr):
                ref[i] = jnp.zeros(ref.shape[1:], ref.dtype)
            return carry

        def zero_state(i, carry):
            h_scr[i] = jnp.zeros(h_scr.shape[1:], h_scr.dtype)
            return carry

        lax.fori_loop(0, x1_scr.shape[0], zero_chain, 0)
        lax.fori_loop(0, h_scr.shape[0], zero_state, 0)

    SL = fm_ref.shape[1]
    L = CHUNK
    cps = SL // L
    W = RWKV_WIDTH
    n = 2 * L
    n_chains = x1_scr.shape[0]

    ri = lax.broadcasted_iota(jnp.int32, (n, n), 0)
    ci = lax.broadcasted_iota(jnp.int32, (n, n), 1)
    eye = ri == ci
    blk = lambda size: (ri // size) == (ci // size)
    blk8, blk16, blk32, blk64 = blk(8), blk(16), blk(32), blk(64)
    stricts = (jnp.logical_and(blk64, ci < ri), jnp.logical_and(blk64, ci > ri))
    chains = [(z, j, p) for z in range(N_DIRS) for j in range(cps) for p in range(N_PAIRS)]
    assert len(chains) == n_chains
    strict = [stricts[z] for z, _, _ in chains]
    incl = [jnp.logical_or(m, eye) for m in strict]
    odds = [z == 0 for z, _, _ in chains]
    blks = (blk8, blk16, blk32, blk64)

    x1 = [x1_scr[i] for i in range(n_chains)]
    x2 = [x2_scr[i] for i in range(n_chains)]
    x3 = [x3_scr[i] for i in range(n_chains)]
    v_s = [v_scr[i] for i in range(n_chains)]
    gam_rows = [gam_scr[i][0:1] for i in range(n_chains)]
    gram = [_bdot_nt(a, b) for a, b in zip(x1, x2)]
    akk_m = [jnp.where(m, g[:n, :n], 0.0) for m, g in zip(strict, gram)]
    akb_m = [jnp.where(m, g[:n, n:], 0.0) for m, g in zip(strict, gram)]
    ark_m = [jnp.where(m, g[n:, :n], 0.0) for m, g in zip(incl, gram)]
    arb_m = [jnp.where(m, g[n:, n:], 0.0) for m, g in zip(incl, gram)]

    t_inv = []
    inverse = _inverse_stages(akb_m, eye, blks, odds, t_inv)
    av = [None] * n_chains

    def apply_values(lo, hi):
        for i in range(lo, hi):
            av[i] = _bdot(jnp.concatenate([akk_m[i], ark_m[i]], axis=0), v_s[i])

    next(inverse, None)
    next(inverse, None)

    taps = taps_ref[...]
    row8 = lax.broadcasted_iota(jnp.int32, (SUBLANES, 1), 0)

    def shifted(m_ref, p_ref, n_ref, block):
        main = m_ref[0]
        prev_row = jnp.where(block > 0, p_ref[0][SUBLANES - 1:SUBLANES], 0.0)
        next_row = jnp.where(block < n_blocks - 1, n_ref[0][0:1], 0.0)
        up = pltpu.roll(main, 1, 0)
        dn = pltpu.roll(main, SL - 1, 0)
        up = jnp.concatenate([jnp.where(row8 == 0, prev_row, up[:SUBLANES]), up[SUBLANES:]], axis=0)
        dn = jnp.concatenate([dn[:SL - SUBLANES],
                              jnp.where(row8 == SUBLANES - 1, next_row, dn[SL - SUBLANES:])], axis=0)
        return taps[0:1] * up + taps[1:2] * main + taps[2:3] * dn

    cf = s
    cb = n_blocks - 1 - s
    sh = (shifted(fm_ref, fp_ref, fn_ref, cf), shifted(bm_ref, bp_ref, bn_ref, cb))
    r = [x[:, 0:W] for x in sh]
    k = [x[:, W:2 * W] for x in sh]
    v = [x[:, 2 * W:3 * W] for x in sh]
    wd = [x[:, 3 * W:3 * W + LORA_RANK] for x in sh]
    ad = [x[:, 3 * W + LORA_RANK:3 * W + 2 * LORA_RANK] for x in sh]
    lora_w = [_bdot(jnp.tanh(wd[z]), wup_ref[z]) for z in range(N_DIRS)]
    lora_a = [_bdot(ad[z], aup_ref[z]) for z in range(N_DIRS)]
    lora_a0b = _bdot(ad[1], aup_ref[0])
    kkr = [x * kk_ref[...] for x in k]
    kk_ss = [_bdot(x * x, ones_ref[...]) for x in kkr]

    next(inverse, None)
    next(inverse, None)

    lw = [-DECAY_SCALE * _sigmoid(w0_ref[z] + lora_w[z]) for z in range(N_DIRS)]
    a = [_sigmoid(a0_ref[z] + lora_a[z]) for z in range(N_DIRS)]
    kk = [x * lax.rsqrt(y + L2_EPS) for x, y in zip(kkr, kk_ss)]
    kt = [k[z] * (1.0 + (a[z] - 1.0) * ka_ref[...]) for z in range(N_DIRS)]
    akk = [a[z] * kk[z] for z in range(N_DIRS)]
    lw_hi = [x.astype(BF16) for x in lw]
    lw_lo = [(x - y.astype(F32)).astype(BF16) for x, y in zip(lw, lw_hi)]
    g = [jnp.dot(tri_ref[z], lw_hi[z], preferred_element_type=F32)
         + jnp.dot(tri_ref[z], lw_lo[z], preferred_element_type=F32) for z in range(N_DIRS)]
    a0_b = _sigmoid(a0_ref[0] + lora_a0b)
    kt0_b = k[1] * (1.0 + (a0_b - 1.0) * ka_ref[...])
    rk_sum = r[1] * (kt0_b + kt[1]) * rk_ref[...]
    rk_hi = rk_sum.astype(BF16)
    rk_lo = (rk_sum - rk_hi.astype(F32)).astype(BF16)
    bonus_dots = (jnp.dot(rk_hi, ones_ref[...], preferred_element_type=F32)
                  + jnp.dot(rk_lo, ones_ref[...], preferred_element_type=F32))

    quarter = n_chains // 4
    for part in range(4):
        next(inverse, None)
        apply_values(part * quarter, (part + 1) * quarter)
    next(inverse, None)
    assert len(t_inv) == n_chains
    wu = [_bdot(t, jnp.concatenate([p[:n], q[:n].astype(BF16)], axis=1))
          for t, p, q in zip(t_inv, x1, av)]
    aw = [_bdot(p, q) for p, q in zip(arb_m, wu)]
    qe = [p[n:].astype(F32) - q[:, :PAIR] for p, q in zip(x1, aw)]
    yv = [p[n:] - q[:, PAIR:] for p, q in zip(av, aw)]
    c0 = [_bdot_tn(p, jnp.concatenate([q, (-u[:, PAIR:]).astype(BF16)], axis=0))
          for p, q, u in zip(x3, v_s, wu)]
    bw = [_bdot_tn(p[n:], u[:, :PAIR]) for p, u in zip(x3, wu)]
    m_c = [jnp.where(eye, jnp.broadcast_to(g, (n, PAIR)), 0.0) - q for g, q in zip(gam_rows, bw)]

    lanes = [(z, p) for z in range(N_DIRS) for p in range(N_PAIRS)]
    h = [jnp.where(chains_start_sequence, 0.0, h_scr[z * N_PAIRS + p]) for z, p in lanes]
    ys = [None] * n_chains
    for step in range(cps):
        idx = [chains.index((z, step if z == 0 else cps - 1 - step, p)) for z, p in lanes]
        y_s = [_bdot(qe[i], hh) + yv[i] for i, hh in zip(idx, h)]
        h = [_bdot(m_c[i], hh) + c0[i] for i, hh in zip(idx, h)]
        for i, y in zip(idx, y_s):
            ys[i] = y[:L] + y[L:]
    for (z, p), hh in zip(lanes, h):
        h_scr[z * N_PAIRS + p] = hh

    def assemble(z):
        return jnp.concatenate(
            [jnp.concatenate([ys[chains.index((z, j, p))] for p in range(N_PAIRS)], axis=1)
             for j in range(cps)], axis=0)

    yf_ref[0] = assemble(0).astype(BF16)
    yb_ref[0] = assemble(1).astype(BF16)

    bon_ref[0] = (bonus_dots * v[1]).astype(BF16)
    lane_lo = lax.broadcasted_iota(jnp.int32, (L, PAIR), 1) < HEAD_DIM
    zero = jnp.zeros((L, PAIR), BF16)

    def stack(x, j, p):
        xp = x[j * L:(j + 1) * L, p * PAIR:(p + 1) * PAIR]
        return jnp.concatenate([jnp.where(lane_lo, xp, zero), jnp.where(lane_lo, zero, xp)], axis=0)

    for z in range(N_DIRS):
        g_tot = [jnp.sum(lw[z][j * L:(j + 1) * L], axis=0, keepdims=True) for j in range(cps)]
        g_tot_rows = jnp.concatenate([jnp.broadcast_to(x, (L, W)) for x in g_tot], axis=0)
        e_out = jnp.exp(-g[z])
        e_rem = jnp.exp(g_tot_rows - g[z])
        kkg = (kk[z] * jnp.exp(g[z] - lw[z])).astype(BF16)
        rg = (r[z] * jnp.exp(g[z])).astype(BF16)
        kd = (kt[z] * e_out).astype(BF16)
        bd = (akk[z] * e_out).astype(BF16)
        kdg = (kt[z] * e_rem).astype(BF16)
        bdg = (akk[z] * e_rem).astype(BF16)
        v_bf = v[z].astype(BF16)
        for j in range(cps):
            gam = jnp.broadcast_to(jnp.exp(g_tot[j]), (SUBLANES, W))
            for p in range(N_PAIRS):
                i = chains.index((z, j, p))
                x1_scr[i] = jnp.concatenate([stack(kkg, j, p), stack(rg, j, p)], axis=0)
                x2_scr[i] = jnp.concatenate([stack(kd, j, p), stack(bd, j, p)], axis=0)
                x3_scr[i] = jnp.concatenate([stack(kdg, j, p), stack(bdg, j, p)], axis=0)
                v_scr[i] = stack(v_bf, j, p)
                gam_scr[i] = gam[:, p * PAIR:(p + 1) * PAIR]


def _rwkv_call(rin, taps, tri, w_up, w0, a_up, a0, k_k, k_a, r_k, ones):
    B, T, C = rin.shape
    L = CHUNK * CHUNKS_PER_STEP
    nC = T // L
    hb = L // SUBLANES
    n_hb = T // SUBLANES
    W = RWKV_WIDTH
    n_chains = N_DIRS * CHUNKS_PER_STEP * N_PAIRS

    n_total = B * nC
    def prep(g): return jnp.minimum(g, n_total - 1)
    def chain(g): return jnp.maximum(g - 1, 0)
    def seq(blk): return blk // nC
    def pos_f(blk): return blk % nC
    def pos_b(blk): return nC - 1 - blk % nC
    def main_f(g): return (seq(prep(g)), pos_f(prep(g)), 0)
    def prev_f(g): return (seq(prep(g)), jnp.maximum(pos_f(prep(g)) * hb - 1, 0), 0)
    def next_f(g): return (seq(prep(g)), jnp.minimum((pos_f(prep(g)) + 1) * hb, n_hb - 1), 0)
    def main_b(g): return (seq(prep(g)), pos_b(prep(g)), 0)
    def prev_b(g): return (seq(prep(g)), jnp.maximum(pos_b(prep(g)) * hb - 1, 0), 0)
    def next_b(g): return (seq(prep(g)), jnp.minimum((pos_b(prep(g)) + 1) * hb, n_hb - 1), 0)
    def out_f(g): return (seq(chain(g)), pos_f(chain(g)), 0)
    def out_b(g): return (seq(chain(g)), pos_b(chain(g)), 0)

    const2 = lambda g: (0, 0)
    const3 = lambda g: (0, 0, 0)
    out_shape = tuple(jax.ShapeDtypeStruct((B, T, W), BF16) for _ in range(3))
    n = 2 * CHUNK
    return pl.pallas_call(
        functools.partial(_rwkv_kernel, n_blocks=nC),
        name="rwkv7_chunked",
        grid=(n_total + 1,),
        in_specs=[pl.BlockSpec((1, L, C), main_f),
                  pl.BlockSpec((1, SUBLANES, C), prev_f),
                  pl.BlockSpec((1, SUBLANES, C), next_f),
                  pl.BlockSpec((1, L, C), main_b),
                  pl.BlockSpec((1, SUBLANES, C), prev_b),
                  pl.BlockSpec((1, SUBLANES, C), next_b),
                  pl.BlockSpec((3, C), const2),
                  pl.BlockSpec((N_DIRS, L, L), const3),
                  pl.BlockSpec((N_DIRS, LORA_RANK, W), const3),
                  pl.BlockSpec((N_DIRS, 1, W), const3),
                  pl.BlockSpec((N_DIRS, LORA_RANK, W), const3),
                  pl.BlockSpec((N_DIRS, 1, W), const3),
                  pl.BlockSpec((1, W), const2),
                  pl.BlockSpec((1, W), const2),
                  pl.BlockSpec((1, W), const2),
                  pl.BlockSpec((W, W), const2)],
        out_specs=(pl.BlockSpec((1, L, W), out_f),
                   pl.BlockSpec((1, L, W), out_b),
                   pl.BlockSpec((1, L, W), main_b)),
        out_shape=out_shape,
        scratch_shapes=[pltpu.VMEM((N_DIRS * N_PAIRS, n, PAIR), F32),
                        pltpu.VMEM((n_chains, 2 * n, PAIR), BF16),
                        pltpu.VMEM((n_chains, 2 * n, PAIR), BF16),
                        pltpu.VMEM((n_chains, 2 * n, PAIR), BF16),
                        pltpu.VMEM((n_chains, n, PAIR), BF16),
                        pltpu.VMEM((n_chains, SUBLANES, PAIR), F32)],
        compiler_params=pltpu.CompilerParams(dimension_semantics=("arbitrary",),
                                             vmem_limit_bytes=VMEM_LIMIT),
    )(rin, rin, rin, rin, rin, rin, taps, tri, w_up, w0, a_up, a0, k_k, k_a, r_k, ones)


def _outproj_kernel(att_ref, yf_ref, yb_ref, bon_ref, grw_ref, x_ref, mod_ref, w_ref,
                    gnw_ref, gnb_ref, gpost_ref, ones_ref, o_ref):
    y = yf_ref[0].astype(F32) + yb_ref[0].astype(F32)
    ones = ones_ref[...]
    inv_n = 1.0 / HEAD_DIM
    mu = _bdot(y, ones) * inv_n
    d = y - mu
    var = _bdot(d * d, ones) * inv_n
    yn = d * lax.rsqrt(var + GN_EPS) * gnw_ref[...] + gnb_ref[...]
    g = grw_ref[0].astype(F32)
    rw = ((yn + bon_ref[0].astype(F32)) * (g * _sigmoid(g))).astype(BF16)
    out = (jnp.dot(att_ref[0], w_ref[:ATT_WIDTH, :], preferred_element_type=F32)
           + jnp.dot(rw, w_ref[ATT_WIDTH:, :], preferred_element_type=F32))
    ms = jnp.mean(out * out, axis=-1, keepdims=True)
    on = out * lax.rsqrt(ms + NORM_EPS) * gpost_ref[...]
    o_ref[0] = x_ref[0] + mod_ref[0] * on


def _outproj_call(att, yf, yb, bon, grw, x, mod3, w_out_bf, gn_w, gn_b, g_post, ones, tm):
    B, T, D = x.shape
    W = RWKV_WIDTH
    row = lambda t, b: (b, t, 0)
    const2 = lambda t, b: (0, 0)
    return pl.pallas_call(
        _outproj_kernel,
        name="out_proj",
        grid=(T // tm, B),
        in_specs=[pl.BlockSpec((1, tm, ATT_WIDTH), row),
                  pl.BlockSpec((1, tm, W), row),
                  pl.BlockSpec((1, tm, W), row),
                  pl.BlockSpec((1, tm, W), row),
                  pl.BlockSpec((1, tm, W), row),
                  pl.BlockSpec((1, tm, D), row),
                  pl.BlockSpec((1, 1, D), lambda t, b: (b, 0, 2)),
                  pl.BlockSpec((ATT_WIDTH + W, D), const2),
                  pl.BlockSpec((1, W), const2),
                  pl.BlockSpec((1, W), const2),
                  pl.BlockSpec((1, D), const2),
                  pl.BlockSpec((W, W), const2)],
        out_specs=pl.BlockSpec((1, tm, D), row),
        out_shape=jax.ShapeDtypeStruct((B, T, D), F32),
        compiler_params=pltpu.CompilerParams(dimension_semantics=("arbitrary", "arbitrary"),
                                             vmem_limit_bytes=VMEM_LIMIT),
    )(att, yf, yb, bon, grw, x, mod3, w_out_bf, gn_w, gn_b, g_post, ones)


def _pick_tile(T, target):
    t = min(T, target)
    while T % t:
        t //= 2
    return t


def _permute_heads(w, axis):
    blocks = jnp.split(w, ATT_Q_HEADS, axis=axis)
    return jnp.concatenate([blocks[h] for h in Q_HEAD_ORDER], axis=axis)


def kernel(x, c, w_ada, b_ada, g_pre, w_in, q_norm_g, k_norm_g, shift_taps, w_up, w0, a_up, a0,
           k_k, k_a, r_k, gn_w, gn_b, w_out, g_post):
    B, T, D = x.shape
    depth = w_ada.shape[0]
    assert T % (CHUNK * CHUNKS_PER_STEP) == 0 and T % GRID_W == 0
    tm_in = _pick_tile(T, IN_PROJ_ROWS)
    tm_out = _pick_tile(T, OUT_PROJ_ROWS)
    tq = _pick_tile(T, ATTN_QUERY_ROWS)
    cos, sin = _rope_tables(T)
    ones = _block_ones(RWKV_WIDTH)
    ti = jnp.arange(CHUNK * CHUNKS_PER_STEP)
    same_chunk = (ti[None, :] // CHUNK) == (ti[:, None] // CHUNK)
    tri = jnp.stack([same_chunk & (ti[None, :] <= ti[:, None]),
                     same_chunk & (ti[None, :] >= ti[:, None])]).astype(BF16)
    for l in range(depth):
        mod = _ada_call(c, w_ada[l], b_ada[l])
        mod3 = mod.reshape(B, 1, 3 * D)
        qg = jnp.tile(q_norm_g[l], ATT_Q_HEADS).reshape(1, ATT_WIDTH)
        kg = jnp.tile(k_norm_g[l], ATT_KV_HEADS).reshape(1, ATT_KV_WIDTH)
        wq = _permute_heads(w_in[l][:, Q_OFF:K_OFF], 1).astype(BF16)
        q, k, vt, g_att, rin, g_rw = _inproj_call(
            x, mod3, g_pre[l].reshape(1, D), w_in[l].astype(BF16), wq, qg, kg, cos, sin, ones, tm_in)
        att = _attn_call(q, k, vt, g_att, tq)
        yf, yb, bon = _rwkv_call(
            rin, shift_taps[l], tri, w_up[l].astype(BF16), w0[l].reshape(N_DIRS, 1, RWKV_WIDTH),
            a_up[l].astype(BF16), a0[l].reshape(N_DIRS, 1, RWKV_WIDTH),
            k_k[l].reshape(1, RWKV_WIDTH), k_a[l].reshape(1, RWKV_WIDTH),
            r_k[l].reshape(1, RWKV_WIDTH), ones)
        x = _outproj_call(att, yf, yb, bon, g_rw, x, mod3, w_out[l].astype(BF16),
                          gn_w[l].reshape(1, RWKV_WIDTH), gn_b[l].reshape(1, RWKV_WIDTH),
                          g_post[l].reshape(1, D), ones, tm_out)
    return x
```

```python
import functools

import jax
import jax.numpy as jnp
from jax import lax
from jax.experimental import pallas as pl
from jax.experimental.pallas import tpu as pltpu

F32 = jnp.float32
BF16 = jnp.bfloat16

HEAD_DIM = 64
ATT_Q_HEADS = 8
ATT_KV_HEADS = 2
ATT_GROUP = ATT_Q_HEADS // ATT_KV_HEADS
ATT_WIDTH = ATT_Q_HEADS * HEAD_DIM
ATT_KV_WIDTH = ATT_KV_HEADS * HEAD_DIM
RWKV_HEADS = 8
RWKV_WIDTH = RWKV_HEADS * HEAD_DIM
LORA_RANK = 64
RWKV_SHIFT_WIDTH = 3 * RWKV_WIDTH + 2 * LORA_RANK
N_DIRS = 2
GRID_W = 64
ROPE_THETA = 10000.0
DECAY_SCALE = 0.6065306597126334
NORM_EPS = 1e-6
GN_EPS = 64e-5
L2_EPS = 1e-12

Q_OFF = 0
K_OFF = Q_OFF + ATT_WIDTH
V_OFF = K_OFF + ATT_KV_WIDTH
GATT_OFF = V_OFF + ATT_KV_WIDTH
RIN_OFF = GATT_OFF + ATT_WIDTH
GRW_OFF = RIN_OFF + RWKV_SHIFT_WIDTH
IN_WIDTH = GRW_OFF + RWKV_WIDTH

LANES = 128
SUBLANES = 8
CHUNK = 64
PAIR = 2 * HEAD_DIM
BF16_SUBLANES = 16
VT_ONES = BF16_SUBLANES
VT_KV = HEAD_DIM + VT_ONES
VT_ROWS = ATT_KV_HEADS * VT_KV
Q_SCALE = HEAD_DIM ** -0.5 * 1.4426950408889634
Q_HEAD_ORDER = tuple(g + kv * ATT_GROUP for g in range(ATT_GROUP) for kv in range(ATT_KV_HEADS))
VMEM_LIMIT = 56 * 1024 * 1024
IN_PROJ_ROWS = 256
OUT_PROJ_ROWS = 512
ATTN_QUERY_ROWS = 512
ATTN_KEY_CHUNK = 256
ATTN_MAX_INIT = -1e30


def _bdot(a, b):
    return jnp.dot(a.astype(BF16), b.astype(BF16), preferred_element_type=F32)


def _bdot_nt(a, b):
    return lax.dot_general(a.astype(BF16), b.astype(BF16), (((1,), (1,)), ((), ())),
                           preferred_element_type=F32)


def _bdot_tn(a, b):
    return lax.dot_general(a.astype(BF16), b.astype(BF16), (((0,), (0,)), ((), ())),
                           preferred_element_type=F32)


def _sigmoid(x):
    return 1.0 / (1.0 + jnp.exp(-x))


def _ada_kernel(c_ref, w_ref, b_ref, o_ref):
    c = c_ref[...]
    ca = c * _sigmoid(c)
    o_ref[...] = jnp.dot(ca, w_ref[...], precision=lax.Precision.HIGHEST,
                         preferred_element_type=F32) + b_ref[...]


def _ada_call(c, w_ada, b_ada):
    B, D = c.shape
    n_out = w_ada.shape[1]
    tn = D
    return pl.pallas_call(
        _ada_kernel,
        name="ada_mod",
        grid=(n_out // tn,),
        in_specs=[pl.BlockSpec((B, D), lambda j: (0, 0)),
                  pl.BlockSpec((D, tn), lambda j: (0, j)),
                  pl.BlockSpec((1, tn), lambda j: (0, j))],
        out_specs=pl.BlockSpec((B, tn), lambda j: (0, j)),
        out_shape=jax.ShapeDtypeStruct((B, n_out), F32),
        compiler_params=pltpu.CompilerParams(dimension_semantics=("arbitrary",),
                                             vmem_limit_bytes=VMEM_LIMIT),
    )(c, w_ada, b_ada.reshape(1, n_out))


def _rope(x, cos, sin_signed, first_half):
    n = x.shape[1]
    partner = jnp.where(first_half, pltpu.roll(x, n - 16, 1), pltpu.roll(x, 16, 1))
    return x * cos + partner * sin_signed


def _inproj_kernel(x_ref, mod_ref, gpre_ref, w_ref, wq_ref, qg_ref, kg_ref, cos_ref, sin_ref, ones_ref,
                   q_ref, k_ref, vt_ref, gatt_ref, rin_ref, grw_ref):
    D = x_ref.shape[2]
    x = x_ref[0]
    ms = jnp.mean(x * x, axis=-1, keepdims=True)
    xn = x * lax.rsqrt(ms + NORM_EPS) * gpre_ref[...]
    mod = mod_ref[0]
    shift = mod[:, :D]
    scale = mod[:, D:2 * D]
    h = (xn * (1.0 + scale) + shift).astype(BF16)

    def proj(lo, width):
        return jnp.dot(h, w_ref[:, lo:lo + width], preferred_element_type=F32)

    cos = cos_ref[...]
    sin = sin_ref[...]
    lane = lax.broadcasted_iota(jnp.int32, cos.shape, 1)
    first_half = (lane % 32) < 16

    def mean_sq(y):
        w = y.shape[1]
        return _bdot(y * y, ones_ref[:w, :w]) * (1.0 / HEAD_DIM)

    q = jnp.dot(h, wq_ref[...], preferred_element_type=F32)
    k = proj(K_OFF, ATT_KV_WIDTH)
    v = proj(V_OFF, ATT_KV_WIDTH).astype(BF16)
    rin_ref[0] = proj(RIN_OFF, RWKV_SHIFT_WIDTH)
    q_msq = mean_sq(q)
    k_msq = mean_sq(k)
    eye = (lax.broadcasted_iota(jnp.int32, (ATT_KV_WIDTH, ATT_KV_WIDTH), 0)
           == lax.broadcasted_iota(jnp.int32, (ATT_KV_WIDTH, ATT_KV_WIDTH), 1)).astype(BF16)
    vt = _bdot_nt(eye, v).astype(BF16)
    gatt_ref[0] = proj(GATT_OFF, ATT_WIDTH).astype(BF16)
    grw_ref[0] = proj(GRW_OFF, RWKV_WIDTH).astype(BF16)

    q = q * lax.rsqrt(q_msq + NORM_EPS) * qg_ref[...]
    reps = ATT_WIDTH // LANES
    cos_q = jnp.concatenate([cos] * reps, axis=1)
    sin_q = jnp.concatenate([sin] * reps, axis=1)
    lane_q = lax.broadcasted_iota(jnp.int32, cos_q.shape, 1)
    fh_q = (lane_q % 32) < 16
    q_ref[0] = (_rope(q, cos_q, sin_q, fh_q) * Q_SCALE).astype(BF16)
    k = k * lax.rsqrt(k_msq + NORM_EPS) * kg_ref[...]
    k_ref[0] = _rope(k, cos, sin, first_half).astype(BF16)
    ones_rows = jnp.ones((VT_ONES, vt.shape[1]), BF16)
    vt_ref[0] = jnp.concatenate([vt[:HEAD_DIM], ones_rows, vt[HEAD_DIM:], ones_rows], axis=0)


def _rope_tables(T):
    rows = T // GRID_W
    row = jnp.repeat(jnp.arange(rows, dtype=F32), GRID_W)
    col = jnp.tile(jnp.arange(GRID_W, dtype=F32), rows)
    n_freq = HEAD_DIM // 4
    inv_freq = ROPE_THETA ** (-jnp.arange(n_freq, dtype=F32) / n_freq)
    ang_r = row[:, None] * inv_freq
    ang_c = col[:, None] * inv_freq
    cr, sr, cc, sc = jnp.cos(ang_r), jnp.sin(ang_r), jnp.cos(ang_c), jnp.sin(ang_c)
    cos = jnp.concatenate([cr, cr, cc, cc], axis=1)
    sin = jnp.concatenate([-sr, sr, -sc, sc], axis=1)
    reps = LANES // HEAD_DIM
    return jnp.tile(cos, (1, reps)), jnp.tile(sin, (1, reps))


def _block_ones(n):
    i = jnp.arange(n) // HEAD_DIM
    return (i[:, None] == i[None, :]).astype(BF16)


def _inproj_call(x, mod3, g_pre, w_in_bf, wq_bf, qg, kg, cos, sin, ones, tm):
    B, T, D = x.shape
    grid = (T // tm, B)
    row = lambda t, b: (b, t, 0)
    const2 = lambda t, b: (0, 0)
    out_shapes = (
        jax.ShapeDtypeStruct((B, T, ATT_WIDTH), BF16),
        jax.ShapeDtypeStruct((B, T, ATT_KV_WIDTH), BF16),
        jax.ShapeDtypeStruct((B, VT_ROWS, T), BF16),
        jax.ShapeDtypeStruct((B, T, ATT_WIDTH), BF16),
        jax.ShapeDtypeStruct((B, T, RWKV_SHIFT_WIDTH), F32),
        jax.ShapeDtypeStruct((B, T, RWKV_WIDTH), BF16),
    )
    out_specs = tuple(
        pl.BlockSpec((1, VT_ROWS, tm), lambda t, b: (b, 0, t)) if i == 2
        else pl.BlockSpec((1, tm, s.shape[2]), row) for i, s in enumerate(out_shapes))
    return pl.pallas_call(
        _inproj_kernel,
        name="in_proj",
        grid=grid,
        in_specs=[pl.BlockSpec((1, tm, D), row),
                  pl.BlockSpec((1, 1, mod3.shape[2]), lambda t, b: (b, 0, 0)),
                  pl.BlockSpec((1, D), const2),
                  pl.BlockSpec((D, IN_WIDTH), const2),
                  pl.BlockSpec((D, ATT_WIDTH), const2),
                  pl.BlockSpec((1, ATT_WIDTH), const2),
                  pl.BlockSpec((1, ATT_KV_WIDTH), const2),
                  pl.BlockSpec((tm, LANES), lambda t, b: (t, 0)),
                  pl.BlockSpec((tm, LANES), lambda t, b: (t, 0)),
                  pl.BlockSpec((ATT_WIDTH, ATT_WIDTH), const2)],
        out_specs=out_specs,
        out_shape=out_shapes,
        compiler_params=pltpu.CompilerParams(dimension_semantics=("arbitrary", "arbitrary"),
                                             vmem_limit_bytes=VMEM_LIMIT),
    )(x, mod3, g_pre, w_in_bf, wq_bf, qg, kg, cos, sin, ones)


def _attn_kernel(q_ref, k_ref, vt_ref, g_ref, o_ref):
    q = q_ref[0]
    k = k_ref[0]
    tq = q.shape[0]
    lane_lo = lax.broadcasted_iota(jnp.int32, (tq, PAIR), 1) < HEAD_DIM
    zero = jnp.zeros((tq, PAIR), BF16)
    T = k.shape[0]
    kc = min(T, ATTN_KEY_CHUNK)
    n_kc = T // kc
    qs = []
    for j in range(ATT_GROUP):
        qp = q[:, j * PAIR:(j + 1) * PAIR]
        qs.append(jnp.concatenate([jnp.where(lane_lo, qp, zero), jnp.where(lane_lo, zero, qp)], axis=0))

    def scores(item):
        j, c = item
        return lax.dot_general(k[c * kc:(c + 1) * kc], qs[j], (((1,), (1,)), ((), ())),
                               preferred_element_type=F32)

    items = [(j, c) for j in range(ATT_GROUP) for c in range(n_kc)]
    outs = [None] * ATT_Q_HEADS
    s_next = scores(items[0])
    for idx, (j, c) in enumerate(items):
        s = s_next
        if idx + 1 < len(items):
            s_next = scores(items[idx + 1])
        if c == 0:
            m = jnp.full((1, 2 * tq), ATTN_MAX_INIT, F32)
            acc = [jnp.zeros((VT_KV, tq), F32) for _ in range(ATT_KV_HEADS)]
        m_new = jnp.maximum(m, jnp.max(s, axis=0, keepdims=True))
        alpha = jnp.exp2(m - m_new)
        p = jnp.exp2(s - m_new).astype(BF16)
        m = m_new
        for kv in range(ATT_KV_HEADS):
            vt = vt_ref[0, kv * VT_KV:(kv + 1) * VT_KV, c * kc:(c + 1) * kc]
            acc[kv] = (acc[kv] * alpha[:, kv * tq:(kv + 1) * tq]
                       + jnp.dot(vt, p[:, kv * tq:(kv + 1) * tq], preferred_element_type=F32))
        if c == n_kc - 1:
            for kv in range(ATT_KV_HEADS):
                o = acc[kv]
                outs[Q_HEAD_ORDER[j * ATT_KV_HEADS + kv]] = o[:HEAD_DIM] / o[HEAD_DIM:HEAD_DIM + 1]
    y = jnp.concatenate(outs, axis=0).T
    g = g_ref[0].astype(F32)
    o_ref[0] = (y * (g * _sigmoid(g))).astype(BF16)


def _attn_call(q, k, vt, g_att, tq):
    B, T, _ = q.shape
    return pl.pallas_call(
        _attn_kernel,
        name="gqa_attn",
        grid=(B, T // tq),
        in_specs=[pl.BlockSpec((1, tq, ATT_WIDTH), lambda b, i: (b, i, 0)),
                  pl.BlockSpec((1, T, ATT_KV_WIDTH), lambda b, i: (b, 0, 0)),
                  pl.BlockSpec((1, VT_ROWS, T), lambda b, i: (b, 0, 0)),
                  pl.BlockSpec((1, tq, ATT_WIDTH), lambda b, i: (b, i, 0))],
        out_specs=pl.BlockSpec((1, tq, ATT_WIDTH), lambda b, i: (b, i, 0)),
        out_shape=jax.ShapeDtypeStruct((B, T, ATT_WIDTH), BF16),
        compiler_params=pltpu.CompilerParams(dimension_semantics=("arbitrary", "arbitrary"),
                                             vmem_limit_bytes=VMEM_LIMIT),
    )(q, k, vt, g_att)


CHUNKS_PER_STEP = 4
N_PAIRS = RWKV_WIDTH // PAIR


def _later_rows(x, s, odd):
    first = s if odd else 0
    return jnp.concatenate([x[b:b + s] for b in range(first, x.shape[0], 2 * s)], axis=0)


def _merge_rows(keep, new, s, odd):
    out = []
    for i, b in enumerate(range(0, keep.shape[0], 2 * s)):
        lo, hi = keep[b:b + s], keep[b + s:b + 2 * s]
        blk = new[i * s:(i + 1) * s]
        out += [lo, blk] if odd else [blk, hi]
    return jnp.concatenate(out, axis=0)


INVERSE_STAGES = 9


def _inverse_stages(a_list, eye, blks, odds, out):
    blk8, blk16, blk32, blk64 = blks
    a8 = [jnp.where(blk8, a, 0.0) for a in a_list]
    a2 = [_bdot(x, x) for x in a8]
    yield
    a4 = [_bdot(x, x) for x in a2]
    t = [_bdot(eye - x, eye + y) for x, y in zip(a8, a2)]
    yield
    t = [_bdot(x, eye + y) for x, y in zip(t, a4)]
    for inner, outer, s in ((blk8, blk16, 8), (blk16, blk32, 16), (blk32, blk64, 32)):
        yield
        off = jnp.logical_and(outer, jnp.logical_not(inner))
        xs = [_bdot(_later_rows(x, s, o), jnp.where(off, a, 0.0)) for x, a, o in zip(t, a_list, odds)]
        yield
        upd = [_later_rows(x, s, o) - _bdot(y, x) for x, y, o in zip(t, xs, odds)]
        t = [_merge_rows(x, u, s, o) for x, u, o in zip(t, upd, odds)]
    out.extend(t)


def _rwkv_kernel(fm_ref, fp_ref, fn_ref, bm_ref, bp_ref, bn_ref, taps_ref, tri_ref,
                 wup_ref, w0_ref, aup_ref, a0_ref, kk_ref, ka_ref, rk_ref, ones_ref,
                 yf_ref, yb_ref, bon_ref,
                 h_scr, x1_scr, x2_scr, x3_scr, v_scr, gam_scr, *, n_blocks):
    g_step = pl.program_id(0)
    last_block = pl.num_programs(0) - 2
    s = lax.rem(jnp.minimum(g_step, last_block), jnp.int32(n_blocks))
    chains_start_sequence = lax.rem(jnp.maximum(g_step - 1, 0), jnp.int32(n_blocks)) == 0

    @pl.when(g_step == 0)
    def _():
        def zero_chain(i, carry):
            for ref in (x1_scr, x2_scr, x3_scr, v_scr, gam_scr):
                ref[i] = jnp.zeros(ref.shape[1:], ref.dtype)
            return carry

        def zero_state(i, carry):
            h_scr[i] = jnp.zeros(h_scr.shape[1:], h_scr.dtype)
            return carry

        lax.fori_loop(0, x1_scr.shape[0], zero_chain, 0)
        lax.fori_loop(0, h_scr.shape[0], zero_state, 0)

    SL = fm_ref.shape[1]
    L = CHUNK
    cps = SL // L
    W = RWKV_WIDTH
    n = 2 * L
    n_chains = x1_scr.shape[0]

    ri = lax.broadcasted_iota(jnp.int32, (n, n), 0)
    ci = lax.broadcasted_iota(jnp.int32, (n, n), 1)
    eye = ri == ci
    blk = lambda size: (ri // size) == (ci // size)
    blk8, blk16, blk32, blk64 = blk(8), blk(16), blk(32), blk(64)
    stricts = (jnp.logical_and(blk64, ci < ri), jnp.logical_and(blk64, ci > ri))
    chains = [(z, j, p) for z in range(N_DIRS) for j in range(cps) for p in range(N_PAIRS)]
    assert len(chains) == n_chains
    strict = [stricts[z] for z, _, _ in chains]
    incl = [jnp.logical_or(m, eye) for m in strict]
    odds = [z == 0 for z, _, _ in chains]
    blks = (blk8, blk16, blk32, blk64)

    x1 = [x1_scr[i] for i in range(n_chains)]
    x2 = [x2_scr[i] for i in range(n_chains)]
    x3 = [x3_scr[i] for i in range(n_chains)]
    v_s = [v_scr[i] for i in range(n_chains)]
    gam_rows = [gam_scr[i][0:1] for i in range(n_chains)]
    gram = [_bdot_nt(a, b) for a, b in zip(x1, x2)]
    akk_m = [jnp.where(m, g[:n, :n], 0.0) for m, g in zip(strict, gram)]
    akb_m = [jnp.where(m, g[:n, n:], 0.0) for m, g in zip(strict, gram)]
    ark_m = [jnp.where(m, g[n:, :n], 0.0) for m, g in zip(incl, gram)]
    arb_m = [jnp.where(m, g[n:, n:], 0.0) for m, g in zip(incl, gram)]

    t_inv = []
    inverse = _inverse_stages(akb_m, eye, blks, odds, t_inv)
    av = [None] * n_chains

    def apply_values(lo, hi):
        for i in range(lo, hi):
            av[i] = _bdot(jnp.concatenate([akk_m[i], ark_m[i]], axis=0), v_s[i])

    next(inverse, None)
    next(inverse, None)

    taps = taps_ref[...]
    row8 = lax.broadcasted_iota(jnp.int32, (SUBLANES, 1), 0)

    def shifted(m_ref, p_ref, n_ref, block):
        main = m_ref[0]
        prev_row = jnp.where(block > 0, p_ref[0][SUBLANES - 1:SUBLANES], 0.0)
        next_row = jnp.where(block < n_blocks - 1, n_ref[0][0:1], 0.0)
        up = pltpu.roll(main, 1, 0)
        dn = pltpu.roll(main, SL - 1, 0)
        up = jnp.concatenate([jnp.where(row8 == 0, prev_row, up[:SUBLANES]), up[SUBLANES:]], axis=0)
        dn = jnp.concatenate([dn[:SL - SUBLANES],
                              jnp.where(row8 == SUBLANES - 1, next_row, dn[SL - SUBLANES:])], axis=0)
        return taps[0:1] * up + taps[1:2] * main + taps[2:3] * dn

    cf = s
    cb = n_blocks - 1 - s
    sh = (shifted(fm_ref, fp_ref, fn_ref, cf), shifted(bm_ref, bp_ref, bn_ref, cb))
    r = [x[:, 0:W] for x in sh]
    k = [x[:, W:2 * W] for x in sh]
    v = [x[:, 2 * W:3 * W] for x in sh]
    wd = [x[:, 3 * W:3 * W + LORA_RANK] for x in sh]
    ad = [x[:, 3 * W + LORA_RANK:3 * W + 2 * LORA_RANK] for x in sh]
    lora_w = [_bdot(jnp.tanh(wd[z]), wup_ref[z]) for z in range(N_DIRS)]
    lora_a = [_bdot(ad[z], aup_ref[z]) for z in range(N_DIRS)]
    lora_a0b = _bdot(ad[1], aup_ref[0])
    kkr = [x * kk_ref[...] for x in k]
    kk_ss = [_bdot(x * x, ones_ref[...]) for x in kkr]

    next(inverse, None)
    next(inverse, None)

    lw = [-DECAY_SCALE * _sigmoid(w0_ref[z] + lora_w[z]) for z in range(N_DIRS)]
    a = [_sigmoid(a0_ref[z] + lora_a[z]) for z in range(N_DIRS)]
    kk = [x * lax.rsqrt(y + L2_EPS) for x, y in zip(kkr, kk_ss)]
    kt = [k[z] * (1.0 + (a[z] - 1.0) * ka_ref[...]) for z in range(N_DIRS)]
    akk = [a[z] * kk[z] for z in range(N_DIRS)]
    lw_hi = [x.astype(BF16) for x in lw]
    lw_lo = [(x - y.astype(F32)).astype(BF16) for x, y in zip(lw, lw_hi)]
    g = [jnp.dot(tri_ref[z], lw_hi[z], preferred_element_type=F32)
         + jnp.dot(tri_ref[z], lw_lo[z], preferred_element_type=F32) for z in range(N_DIRS)]
    a0_b = _sigmoid(a0_ref[0] + lora_a0b)
    kt0_b = k[1] * (1.0 + (a0_b - 1.0) * ka_ref[...])
    rk_sum = r[1] * (kt0_b + kt[1]) * rk_ref[...]
    rk_hi = rk_sum.astype(BF16)
    rk_lo = (rk_sum - rk_hi.astype(F32)).astype(BF16)
    bonus_dots = (jnp.dot(rk_hi, ones_ref[...], preferred_element_type=F32)
                  + jnp.dot(rk_lo, ones_ref[...], preferred_element_type=F32))

    quarter = n_chains // 4
    for part in range(4):
        next(inverse, None)
        apply_values(part * quarter, (part + 1) * quarter)
    next(inverse, None)
    assert len(t_inv) == n_chains
    wu = [_bdot(t, jnp.concatenate([p[:n], q[:n].astype(BF16)], axis=1))
          for t, p, q in zip(t_inv, x1, av)]
    aw = [_bdot(p, q) for p, q in zip(arb_m, wu)]
    qe = [p[n:].astype(F32) - q[:, :PAIR] for p, q in zip(x1, aw)]
    yv = [p[n:] - q[:, PAIR:] for p, q in zip(av, aw)]
    c0 = [_bdot_tn(p, jnp.concatenate([q, (-u[:, PAIR:]).astype(BF16)], axis=0))
          for p, q, u in zip(x3, v_s, wu)]
    bw = [_bdot_tn(p[n:], u[:, :PAIR]) for p, u in zip(x3, wu)]
    m_c = [jnp.where(eye, jnp.broadcast_to(g, (n, PAIR)), 0.0) - q for g, q in zip(gam_rows, bw)]

    lanes = [(z, p) for z in range(N_DIRS) for p in range(N_PAIRS)]
    h = [jnp.where(chains_start_sequence, 0.0, h_scr[z * N_PAIRS + p]) for z, p in lanes]
    ys = [None] * n_chains
    for step in range(cps):
        idx = [chains.index((z, step if z == 0 else cps - 1 - step, p)) for z, p in lanes]
        y_s = [_bdot(qe[i], hh) + yv[i] for i, hh in zip(idx, h)]
        h = [_bdot(m_c[i], hh) + c0[i] for i, hh in zip(idx, h)]
        for i, y in zip(idx, y_s):
            ys[i] = y[:L] + y[L:]
    for (z, p), hh in zip(lanes, h):
        h_scr[z * N_PAIRS + p] = hh

    def assemble(z):
        return jnp.concatenate(
            [jnp.concatenate([ys[chains.index((z, j, p))] for p in range(N_PAIRS)], axis=1)
             for j in range(cps)], axis=0)

    yf_ref[0] = assemble(0).astype(BF16)
    yb_ref[0] = assemble(1).astype(BF16)

    bon_ref[0] = (bonus_dots * v[1]).astype(BF16)
    lane_lo = lax.broadcasted_iota(jnp.int32, (L, PAIR), 1) < HEAD_DIM
    zero = jnp.zeros((L, PAIR), BF16)

    def stack(x, j, p):
        xp = x[j * L:(j + 1) * L, p * PAIR:(p + 1) * PAIR]
        return jnp.concatenate([jnp.where(lane_lo, xp, zero), jnp.where(lane_lo, zero, xp)], axis=0)

    for z in range(N_DIRS):
        g_tot = [jnp.sum(lw[z][j * L:(j + 1) * L], axis=0, keepdims=True) for j in range(cps)]
        g_tot_rows = jnp.concatenate([jnp.broadcast_to(x, (L, W)) for x in g_tot], axis=0)
        e_out = jnp.exp(-g[z])
        e_rem = jnp.exp(g_tot_rows - g[z])
        kkg = (kk[z] * jnp.exp(g[z] - lw[z])).astype(BF16)
        rg = (r[z] * jnp.exp(g[z])).astype(BF16)
        kd = (kt[z] * e_out).astype(BF16)
        bd = (akk[z] * e_out).astype(BF16)
        kdg = (kt[z] * e_rem).astype(BF16)
        bdg = (akk[z] * e_rem).astype(BF16)
        v_bf = v[z].astype(BF16)
        for j in range(cps):
            gam = jnp.broadcast_to(jnp.exp(g_tot[j]), (SUBLANES, W))
            for p in range(N_PAIRS):
                i = chains.index((z, j, p))
                x1_scr[i] = jnp.concatenate([stack(kkg, j, p), stack(rg, j, p)], axis=0)
                x2_scr[i] = jnp.concatenate([stack(kd, j, p), stack(bd, j, p)], axis=0)
                x3_scr[i] = jnp.concatenate([stack(kdg, j, p), stack(bdg, j, p)], axis=0)
                v_scr[i] = stack(v_bf, j, p)
                gam_scr[i] = gam[:, p * PAIR:(p + 1) * PAIR]


def _rwkv_call(rin, taps, tri, w_up, w0, a_up, a0, k_k, k_a, r_k, ones):
    B, T, C = rin.shape
    L = CHUNK * CHUNKS_PER_STEP
    nC = T // L
    hb = L // SUBLANES
    n_hb = T // SUBLANES
    W = RWKV_WIDTH
    n_chains = N_DIRS * CHUNKS_PER_STEP * N_PAIRS

    n_total = B * nC
    def prep(g): return jnp.minimum(g, n_total - 1)
    def chain(g): return jnp.maximum(g - 1, 0)
    def seq(blk): return blk // nC
    def pos_f(blk): return blk % nC
    def pos_b(blk): return nC - 1 - blk % nC
    def main_f(g): return (seq(prep(g)), pos_f(prep(g)), 0)
    def prev_f(g): return (seq(prep(g)), jnp.maximum(pos_f(prep(g)) * hb - 1, 0), 0)
    def next_f(g): return (seq(prep(g)), jnp.minimum((pos_f(prep(g)) + 1) * hb, n_hb - 1), 0)
    def main_b(g): return (seq(prep(g)), pos_b(prep(g)), 0)
    def prev_b(g): return (seq(prep(g)), jnp.maximum(pos_b(prep(g)) * hb - 1, 0), 0)
    def next_b(g): return (seq(prep(g)), jnp.minimum((pos_b(prep(g)) + 1) * hb, n_hb - 1), 0)
    def out_f(g): return (seq(chain(g)), pos_f(chain(g)), 0)
    def out_b(g): return (seq(chain(g)), pos_b(chain(g)), 0)

    const2 = lambda g: (0, 0)
    const3 = lambda g: (0, 0, 0)
    out_shape = tuple(jax.ShapeDtypeStruct((B, T, W), BF16) for _ in range(3))
    n = 2 * CHUNK
    return pl.pallas_call(
        functools.partial(_rwkv_kernel, n_blocks=nC),
        name="rwkv7_chunked",
        grid=(n_total + 1,),
        in_specs=[pl.BlockSpec((1, L, C), main_f),
                  pl.BlockSpec((1, SUBLANES, C), prev_f),
                  pl.BlockSpec((1, SUBLANES, C), next_f),
                  pl.BlockSpec((1, L, C), main_b),
                  pl.BlockSpec((1, SUBLANES, C), prev_b),
                  pl.BlockSpec((1, SUBLANES, C), next_b),
                  pl.BlockSpec((3, C), const2),
                  pl.BlockSpec((N_DIRS, L, L), const3),
                  pl.BlockSpec((N_DIRS, LORA_RANK, W), const3),
                  pl.BlockSpec((N_DIRS, 1, W), const3),
                  pl.BlockSpec((N_DIRS, LORA_RANK, W), const3),
                  pl.BlockSpec((N_DIRS, 1, W), const3),
                  pl.BlockSpec((1, W), const2),
                  pl.BlockSpec((1, W), const2),
                  pl.BlockSpec((1, W), const2),
                  pl.BlockSpec((W, W), const2)],
        out_specs=(pl.BlockSpec((1, L, W), out_f),
                   pl.BlockSpec((1, L, W), out_b),
                   pl.BlockSpec((1, L, W), main_b)),
        out_shape=out_shape,
        scratch_shapes=[pltpu.VMEM((N_DIRS * N_PAIRS, n, PAIR), F32),
                        pltpu.VMEM((n_chains, 2 * n, PAIR), BF16),
                        pltpu.VMEM((n_chains, 2 * n, PAIR), BF16),
                        pltpu.VMEM((n_chains, 2 * n, PAIR), BF16),
                        pltpu.VMEM((n_chains, n, PAIR), BF16),
                        pltpu.VMEM((n_chains, SUBLANES, PAIR), F32)],
        compiler_params=pltpu.CompilerParams(dimension_semantics=("arbitrary",),
                                             vmem_limit_bytes=VMEM_LIMIT),
    )(rin, rin, rin, rin, rin, rin, taps, tri, w_up, w0, a_up, a0, k_k, k_a, r_k, ones)


def _outproj_kernel(att_ref, yf_ref, yb_ref, bon_ref, grw_ref, x_ref, mod_ref, w_ref,
                    gnw_ref, gnb_ref, gpost_ref, ones_ref, o_ref):
    y = yf_ref[0].astype(F32) + yb_ref[0].astype(F32)
    ones = ones_ref[...]
    inv_n = 1.0 / HEAD_DIM
    mu = _bdot(y, ones) * inv_n
    d = y - mu
    var = _bdot(d * d, ones) * inv_n
    yn = d * lax.rsqrt(var + GN_EPS) * gnw_ref[...] + gnb_ref[...]
    g = grw_ref[0].astype(F32)
    rw = ((yn + bon_ref[0].astype(F32)) * (g * _sigmoid(g))).astype(BF16)
    out = (jnp.dot(att_ref[0], w_ref[:ATT_WIDTH, :], preferred_element_type=F32)
           + jnp.dot(rw, w_ref[ATT_WIDTH:, :], preferred_element_type=F32))
    ms = jnp.mean(out * out, axis=-1, keepdims=True)
    on = out * lax.rsqrt(ms + NORM_EPS) * gpost_ref[...]
    o_ref[0] = x_ref[0] + mod_ref[0] * on


def _outproj_call(att, yf, yb, bon, grw, x, mod3, w_out_bf, gn_w, gn_b, g_post, ones, tm):
    B, T, D = x.shape
    W = RWKV_WIDTH
    row = lambda t, b: (b, t, 0)
    const2 = lambda t, b: (0, 0)
    return pl.pallas_call(
        _outproj_kernel,
        name="out_proj",
        grid=(T // tm, B),
        in_specs=[pl.BlockSpec((1, tm, ATT_WIDTH), row),
                  pl.BlockSpec((1, tm, W), row),
                  pl.BlockSpec((1, tm, W), row),
                  pl.BlockSpec((1, tm, W), row),
                  pl.BlockSpec((1, tm, W), row),
                  pl.BlockSpec((1, tm, D), row),
                  pl.BlockSpec((1, 1, D), lambda t, b: (b, 0, 2)),
                  pl.BlockSpec((ATT_WIDTH + W, D), const2),
                  pl.BlockSpec((1, W), const2),
                  pl.BlockSpec((1, W), const2),
                  pl.BlockSpec((1, D), const2),
                  pl.BlockSpec((W, W), const2)],
        out_specs=pl.BlockSpec((1, tm, D), row),
        out_shape=jax.ShapeDtypeStruct((B, T, D), F32),
        compiler_params=pltpu.CompilerParams(dimension_semantics=("arbitrary", "arbitrary"),
                                             vmem_limit_bytes=VMEM_LIMIT),
    )(att, yf, yb, bon, grw, x, mod3, w_out_bf, gn_w, gn_b, g_post, ones)


def _pick_tile(T, target):
    t = min(T, target)
    while T % t:
        t //= 2
    return t


def _permute_heads(w, axis):
    blocks = jnp.split(w, ATT_Q_HEADS, axis=axis)
    return jnp.concatenate([blocks[h] for h in Q_HEAD_ORDER], axis=axis)


def kernel(x, c, w_ada, b_ada, g_pre, w_in, q_norm_g, k_norm_g, shift_taps, w_up, w0, a_up, a0,
           k_k, k_a, r_k, gn_w, gn_b, w_out, g_post):
    B, T, D = x.shape
    depth = w_ada.shape[0]
    assert T % (CHUNK * CHUNKS_PER_STEP) == 0 and T % GRID_W == 0
    tm_in = _pick_tile(T, IN_PROJ_ROWS)
    tm_out = _pick_tile(T, OUT_PROJ_ROWS)
    tq = _pick_tile(T, ATTN_QUERY_ROWS)
    cos, sin = _rope_tables(T)
    ones = _block_ones(RWKV_WIDTH)
    ti = jnp.arange(CHUNK * CHUNKS_PER_STEP)
    same_chunk = (ti[None, :] // CHUNK) == (ti[:, None] // CHUNK)
    tri = jnp.stack([same_chunk & (ti[None, :] <= ti[:, None]),
                     same_chunk & (ti[None, :] >= ti[:, None])]).astype(BF16)
    for l in range(depth):
        mod = _ada_call(c, w_ada[l], b_ada[l])
        mod3 = mod.reshape(B, 1, 3 * D)
        qg = jnp.tile(q_norm_g[l], ATT_Q_HEADS).reshape(1, ATT_WIDTH)
        kg = jnp.tile(k_norm_g[l], ATT_KV_HEADS).reshape(1, ATT_KV_WIDTH)
        wq = _permute_heads(w_in[l][:, Q_OFF:K_OFF], 1).astype(BF16)
        q, k, vt, g_att, rin, g_rw = _inproj_call(
            x, mod3, g_pre[l].reshape(1, D), w_in[l].astype(BF16), wq, qg, kg, cos, sin, ones, tm_in)
        att = _attn_call(q, k, vt, g_att, tq)
        yf, yb, bon = _rwkv_call(
            rin, shift_taps[l], tri, w_up[l].astype(BF16), w0[l].reshape(N_DIRS, 1, RWKV_WIDTH),
            a_up[l].astype(BF16), a0[l].reshape(N_DIRS, 1, RWKV_WIDTH),
            k_k[l].reshape(1, RWKV_WIDTH), k_a[l].reshape(1, RWKV_WIDTH),
            r_k[l].reshape(1, RWKV_WIDTH), ones)
        x = _outproj_call(att, yf, yb, bon, g_rw, x, mod3, w_out[l].astype(BF16),
                          gn_w[l].reshape(1, RWKV_WIDTH), gn_b[l].reshape(1, RWKV_WIDTH),
                          g_post[l].reshape(1, D), ones, tm_out)
    return x
```

```python
import functools

import jax
import jax.numpy as jnp
from jax import lax
from jax.experimental import pallas as pl
from jax.experimental.pallas import tpu as pltpu

F32 = jnp.float32
BF16 = jnp.bfloat16

HEAD_DIM = 64
ATT_Q_HEADS = 8
ATT_KV_HEADS = 2
ATT_GROUP = ATT_Q_HEADS // ATT_KV_HEADS
ATT_WIDTH = ATT_Q_HEADS * HEAD_DIM
ATT_KV_WIDTH = ATT_KV_HEADS * HEAD_DIM
RWKV_HEADS = 8
RWKV_WIDTH = RWKV_HEADS * HEAD_DIM
LORA_RANK = 64
RWKV_SHIFT_WIDTH = 3 * RWKV_WIDTH + 2 * LORA_RANK
N_DIRS = 2
GRID_W = 64
ROPE_THETA = 10000.0
DECAY_SCALE = 0.6065306597126334
NORM_EPS = 1e-6
GN_EPS = 64e-5
L2_EPS = 1e-12

Q_OFF = 0
K_OFF = Q_OFF + ATT_WIDTH
V_OFF = K_OFF + ATT_KV_WIDTH
GATT_OFF = V_OFF + ATT_KV_WIDTH
RIN_OFF = GATT_OFF + ATT_WIDTH
GRW_OFF = RIN_OFF + RWKV_SHIFT_WIDTH
IN_WIDTH = GRW_OFF + RWKV_WIDTH

LANES = 128
SUBLANES = 8
CHUNK = 64
PAIR = 2 * HEAD_DIM
BF16_SUBLANES = 16
VT_ONES = BF16_SUBLANES
VT_KV = HEAD_DIM + VT_ONES
VT_ROWS = ATT_KV_HEADS * VT_KV
Q_SCALE = HEAD_DIM ** -0.5 * 1.4426950408889634
Q_HEAD_ORDER = tuple(g + kv * ATT_GROUP for g in range(ATT_GROUP) for kv in range(ATT_KV_HEADS))
VMEM_LIMIT = 56 * 1024 * 1024
IN_PROJ_ROWS = 256
OUT_PROJ_ROWS = 512
ATTN_QUERY_ROWS = 512
ATTN_KEY_CHUNK = 256
ATTN_SCORES_AHEAD = 1
ATTN_MAX_INIT = -1e30


def _bdot(a, b):
    return jnp.dot(a.astype(BF16), b.astype(BF16), preferred_element_type=F32)


def _bdot_nt(a, b):
    return lax.dot_general(a.astype(BF16), b.astype(BF16), (((1,), (1,)), ((), ())),
                           preferred_element_type=F32)


def _bdot_tn(a, b):
    return lax.dot_general(a.astype(BF16), b.astype(BF16), (((0,), (0,)), ((), ())),
                           preferred_element_type=F32)


def _sigmoid(x):
    return 1.0 / (1.0 + jnp.exp(-x))


def _ada_kernel(c_ref, w_ref, b_ref, o_ref):
    c = c_ref[...]
    ca = c * _sigmoid(c)
    o_ref[...] = jnp.dot(ca, w_ref[...], precision=lax.Precision.HIGHEST,
                         preferred_element_type=F32) + b_ref[...]


def _ada_call(c, w_ada, b_ada):
    B, D = c.shape
    n_out = w_ada.shape[1]
    tn = D
    return pl.pallas_call(
        _ada_kernel,
        name="ada_mod",
        grid=(n_out // tn,),
        in_specs=[pl.BlockSpec((B, D), lambda j: (0, 0)),
                  pl.BlockSpec((D, tn), lambda j: (0, j)),
                  pl.BlockSpec((1, tn), lambda j: (0, j))],
        out_specs=pl.BlockSpec((B, tn), lambda j: (0, j)),
        out_shape=jax.ShapeDtypeStruct((B, n_out), F32),
        compiler_params=pltpu.CompilerParams(dimension_semantics=("arbitrary",),
                                             vmem_limit_bytes=VMEM_LIMIT),
    )(c, w_ada, b_ada.reshape(1, n_out))


def _rope(x, cos, sin_signed, first_half):
    n = x.shape[1]
    partner = jnp.where(first_half, pltpu.roll(x, n - 16, 1), pltpu.roll(x, 16, 1))
    return x * cos + partner * sin_signed


def _inproj_kernel(x_ref, mod_ref, gpre_ref, w_ref, wq_ref, qg_ref, kg_ref, cos_ref, sin_ref, ones_ref,
                   q_ref, k_ref, vt_ref, gatt_ref, rin_ref, grw_ref):
    D = x_ref.shape[2]
    x = x_ref[0]
    ms = jnp.mean(x * x, axis=-1, keepdims=True)
    xn = x * lax.rsqrt(ms + NORM_EPS) * gpre_ref[...]
    mod = mod_ref[0]
    shift = mod[:, :D]
    scale = mod[:, D:2 * D]
    h = (xn * (1.0 + scale) + shift).astype(BF16)

    def proj(lo, width):
        return jnp.dot(h, w_ref[:, lo:lo + width], preferred_element_type=F32)

    cos = cos_ref[...]
    sin = sin_ref[...]
    lane = lax.broadcasted_iota(jnp.int32, cos.shape, 1)
    first_half = (lane % 32) < 16

    def mean_sq(y):
        w = y.shape[1]
        return _bdot(y * y, ones_ref[:w, :w]) * (1.0 / HEAD_DIM)

    q = jnp.dot(h, wq_ref[...], preferred_element_type=F32)
    k = proj(K_OFF, ATT_KV_WIDTH)
    v = proj(V_OFF, ATT_KV_WIDTH).astype(BF16)
    rin_ref[0] = proj(RIN_OFF, RWKV_SHIFT_WIDTH)
    q_msq = mean_sq(q)
    k_msq = mean_sq(k)
    eye = (lax.broadcasted_iota(jnp.int32, (ATT_KV_WIDTH, ATT_KV_WIDTH), 0)
           == lax.broadcasted_iota(jnp.int32, (ATT_KV_WIDTH, ATT_KV_WIDTH), 1)).astype(BF16)
    vt = _bdot_nt(eye, v).astype(BF16)
    gatt_ref[0] = proj(GATT_OFF, ATT_WIDTH).astype(BF16)
    grw_ref[0] = proj(GRW_OFF, RWKV_WIDTH).astype(BF16)

    q = q * lax.rsqrt(q_msq + NORM_EPS) * qg_ref[...]
    reps = ATT_WIDTH // LANES
    cos_q = jnp.concatenate([cos] * reps, axis=1)
    sin_q = jnp.concatenate([sin] * reps, axis=1)
    lane_q = lax.broadcasted_iota(jnp.int32, cos_q.shape, 1)
    fh_q = (lane_q % 32) < 16
    q_ref[0] = (_rope(q, cos_q, sin_q, fh_q) * Q_SCALE).astype(BF16)
    k = k * lax.rsqrt(k_msq + NORM_EPS) * kg_ref[...]
    k_ref[0] = _rope(k, cos, sin, first_half).astype(BF16)
    ones_rows = jnp.ones((VT_ONES, vt.shape[1]), BF16)
    vt_ref[0] = jnp.concatenate([vt[:HEAD_DIM], ones_rows, vt[HEAD_DIM:], ones_rows], axis=0)


def _rope_tables(T):
    rows = T // GRID_W
    row = jnp.repeat(jnp.arange(rows, dtype=F32), GRID_W)
    col = jnp.tile(jnp.arange(GRID_W, dtype=F32), rows)
    n_freq = HEAD_DIM // 4
    inv_freq = ROPE_THETA ** (-jnp.arange(n_freq, dtype=F32) / n_freq)
    ang_r = row[:, None] * inv_freq
    ang_c = col[:, None] * inv_freq
    cr, sr, cc, sc = jnp.cos(ang_r), jnp.sin(ang_r), jnp.cos(ang_c), jnp.sin(ang_c)
    cos = jnp.concatenate([cr, cr, cc, cc], axis=1)
    sin = jnp.concatenate([-sr, sr, -sc, sc], axis=1)
    reps = LANES // HEAD_DIM
    return jnp.tile(cos, (1, reps)), jnp.tile(sin, (1, reps))


def _block_ones(n):
    i = jnp.arange(n) // HEAD_DIM
    return (i[:, None] == i[None, :]).astype(BF16)


def _inproj_call(x, mod3, g_pre, w_in_bf, wq_bf, qg, kg, cos, sin, ones, tm):
    B, T, D = x.shape
    grid = (T // tm, B)
    row = lambda t, b: (b, t, 0)
    const2 = lambda t, b: (0, 0)
    out_shapes = (
        jax.ShapeDtypeStruct((B, T, ATT_WIDTH), BF16),
        jax.ShapeDtypeStruct((B, T, ATT_KV_WIDTH), BF16),
        jax.ShapeDtypeStruct((B, VT_ROWS, T), BF16),
        jax.ShapeDtypeStruct((B, T, ATT_WIDTH), BF16),
        jax.ShapeDtypeStruct((B, T, RWKV_SHIFT_WIDTH), F32),
        jax.ShapeDtypeStruct((B, T, RWKV_WIDTH), BF16),
    )
    out_specs = tuple(
        pl.BlockSpec((1, VT_ROWS, tm), lambda t, b: (b, 0, t)) if i == 2
        else pl.BlockSpec((1, tm, s.shape[2]), row) for i, s in enumerate(out_shapes))
    return pl.pallas_call(
        _inproj_kernel,
        name="in_proj",
        grid=grid,
        in_specs=[pl.BlockSpec((1, tm, D), row),
                  pl.BlockSpec((1, 1, mod3.shape[2]), lambda t, b: (b, 0, 0)),
                  pl.BlockSpec((1, D), const2),
                  pl.BlockSpec((D, IN_WIDTH), const2),
                  pl.BlockSpec((D, ATT_WIDTH), const2),
                  pl.BlockSpec((1, ATT_WIDTH), const2),
                  pl.BlockSpec((1, ATT_KV_WIDTH), const2),
                  pl.BlockSpec((tm, LANES), lambda t, b: (t, 0)),
                  pl.BlockSpec((tm, LANES), lambda t, b: (t, 0)),
                  pl.BlockSpec((ATT_WIDTH, ATT_WIDTH), const2)],
        out_specs=out_specs,
        out_shape=out_shapes,
        compiler_params=pltpu.CompilerParams(dimension_semantics=("arbitrary", "arbitrary"),
                                             vmem_limit_bytes=VMEM_LIMIT),
    )(x, mod3, g_pre, w_in_bf, wq_bf, qg, kg, cos, sin, ones)


def _attn_kernel(q_ref, k_ref, vt_ref, g_ref, o_ref):
    q = q_ref[0]
    k = k_ref[0]
    tq = q.shape[0]
    lane_lo = lax.broadcasted_iota(jnp.int32, (tq, PAIR), 1) < HEAD_DIM
    zero = jnp.zeros((tq, PAIR), BF16)
    T = k.shape[0]
    kc = min(T, ATTN_KEY_CHUNK)
    n_kc = T // kc
    qs = []
    for j in range(ATT_GROUP):
        qp = q[:, j * PAIR:(j + 1) * PAIR]
        qs.append(jnp.concatenate([jnp.where(lane_lo, qp, zero), jnp.where(lane_lo, zero, qp)], axis=0))

    def scores(item):
        j, c = item
        return lax.dot_general(k[c * kc:(c + 1) * kc], qs[j], (((1,), (1,)), ((), ())),
                               preferred_element_type=F32)

    items = [(j, c) for j in range(ATT_GROUP) for c in range(n_kc)]
    outs = [None] * ATT_Q_HEADS
    pending = [scores(item) for item in items[:ATTN_SCORES_AHEAD]]
    for idx, (j, c) in enumerate(items):
        s = pending.pop(0)
        if idx + ATTN_SCORES_AHEAD < len(items):
            pending.append(scores(items[idx + ATTN_SCORES_AHEAD]))
        if c == 0:
            m = jnp.full((1, 2 * tq), ATTN_MAX_INIT, F32)
            acc = [jnp.zeros((VT_KV, tq), F32) for _ in range(ATT_KV_HEADS)]
        m_new = jnp.maximum(m, jnp.max(s, axis=0, keepdims=True))
        alpha = jnp.exp2(m - m_new)
        p = jnp.exp2(s - m_new).astype(BF16)
        m = m_new
        for kv in range(ATT_KV_HEADS):
            vt = vt_ref[0, kv * VT_KV:(kv + 1) * VT_KV, c * kc:(c + 1) * kc]
            acc[kv] = (acc[kv] * alpha[:, kv * tq:(kv + 1) * tq]
                       + jnp.dot(vt, p[:, kv * tq:(kv + 1) * tq], preferred_element_type=F32))
        if c == n_kc - 1:
            for kv in range(ATT_KV_HEADS):
                o = acc[kv]
                outs[Q_HEAD_ORDER[j * ATT_KV_HEADS + kv]] = o[:HEAD_DIM] / o[HEAD_DIM:HEAD_DIM + 1]
    y = jnp.concatenate(outs, axis=0).T
    g = g_ref[0].astype(F32)
    o_ref[0] = (y * (g * _sigmoid(g))).astype(BF16)


def _attn_call(q, k, vt, g_att, tq):
    B, T, _ = q.shape
    return pl.pallas_call(
        _attn_kernel,
        name="gqa_attn",
        grid=(B, T // tq),
        in_specs=[pl.BlockSpec((1, tq, ATT_WIDTH), lambda b, i: (b, i, 0)),
                  pl.BlockSpec((1, T, ATT_KV_WIDTH), lambda b, i: (b, 0, 0)),
                  pl.BlockSpec((1, VT_ROWS, T), lambda b, i: (b, 0, 0)),
                  pl.BlockSpec((1, tq, ATT_WIDTH), lambda b, i: (b, i, 0))],
        out_specs=pl.BlockSpec((1, tq, ATT_WIDTH), lambda b, i: (b, i, 0)),
        out_shape=jax.ShapeDtypeStruct((B, T, ATT_WIDTH), BF16),
        compiler_params=pltpu.CompilerParams(dimension_semantics=("arbitrary", "arbitrary"),
                                             vmem_limit_bytes=VMEM_LIMIT),
    )(q, k, vt, g_att)


CHUNKS_PER_STEP = 4
N_PAIRS = RWKV_WIDTH // PAIR


def _later_rows(x, s, odd):
    first = s if odd else 0
    return jnp.concatenate([x[b:b + s] for b in range(first, x.shape[0], 2 * s)], axis=0)


def _merge_rows(keep, new, s, odd):
    out = []
    for i, b in enumerate(range(0, keep.shape[0], 2 * s)):
        lo, hi = keep[b:b + s], keep[b + s:b + 2 * s]
        blk = new[i * s:(i + 1) * s]
        out += [lo, blk] if odd else [blk, hi]
    return jnp.concatenate(out, axis=0)


INVERSE_STAGES = 9


def _inverse_stages(a_list, eye, blks, odds, out):
    blk8, blk16, blk32, blk64 = blks
    a8 = [jnp.where(blk8, a, 0.0) for a in a_list]
    a2 = [_bdot(x, x) for x in a8]
    yield
    a4 = [_bdot(x, x) for x in a2]
    t = [_bdot(eye - x, eye + y) for x, y in zip(a8, a2)]
    yield
    t = [_bdot(x, eye + y) for x, y in zip(t, a4)]
    for inner, outer, s in ((blk8, blk16, 8), (blk16, blk32, 16), (blk32, blk64, 32)):
        yield
        off = jnp.logical_and(outer, jnp.logical_not(inner))
        xs = [_bdot(_later_rows(x, s, o), jnp.where(off, a, 0.0)) for x, a, o in zip(t, a_list, odds)]
        yield
        upd = [_later_rows(x, s, o) - _bdot(y, x) for x, y, o in zip(t, xs, odds)]
        t = [_merge_rows(x, u, s, o) for x, u, o in zip(t, upd, odds)]
    out.extend(t)


def _rwkv_kernel(fm_ref, fp_ref, fn_ref, bm_ref, bp_ref, bn_ref, taps_ref, tri_ref,
                 wup_ref, w0_ref, aup_ref, a0_ref, kk_ref, ka_ref, rk_ref, ones_ref,
                 yf_ref, yb_ref, bon_ref,
                 h_scr, x1_scr, x2_scr, x3_scr, v_scr, gam_scr, *, n_blocks):
    g_step = pl.program_id(0)
    last_block = pl.num_programs(0) - 2
    s = lax.rem(jnp.minimum(g_step, last_block), jnp.int32(n_blocks))
    chains_start_sequence = lax.rem(jnp.maximum(g_step - 1, 0), jnp.int32(n_blocks)) == 0

    @pl.when(g_step == 0)
    def _():
        def zero_chain(i, carry):
            for ref in (x1_scr, x2_scr, x3_scr, v_scr, gam_scr):
                ref[i] = jnp.zeros(ref.shape[1:], ref.dtype)
            return carry

        def zero_state(i, carry):
            h_scr[i] = jnp.zeros(h_scr.shape[1:], h_scr.dtype)
            return carry

        lax.fori_loop(0, x1_scr.shape[0], zero_chain, 0)
        lax.fori_loop(0, h_scr.shape[0], zero_state, 0)

    SL = fm_ref.shape[1]
    L = CHUNK
    cps = SL // L
    W = RWKV_WIDTH
    n = 2 * L
    n_chains = x1_scr.shape[0]

    ri = lax.broadcasted_iota(jnp.int32, (n, n), 0)
    ci = lax.broadcasted_iota(jnp.int32, (n, n), 1)
    eye = ri == ci
    blk = lambda size: (ri // size) == (ci // size)
    blk8, blk16, blk32, blk64 = blk(8), blk(16), blk(32), blk(64)
    stricts = (jnp.logical_and(blk64, ci < ri), jnp.logical_and(blk64, ci > ri))
    chains = [(z, j, p) for z in range(N_DIRS) for j in range(cps) for p in range(N_PAIRS)]
    assert len(chains) == n_chains
    strict = [stricts[z] for z, _, _ in chains]
    incl = [jnp.logical_or(m, eye) for m in strict]
    odds = [z == 0 for z, _, _ in chains]
    blks = (blk8, blk16, blk32, blk64)

    x1 = [x1_scr[i] for i in range(n_chains)]
    twice = lambda x: jnp.concatenate([x, x], axis=0)
    x2 = [x2_scr[i] for i in range(n_chains)]
    x3 = [x3_scr[i] for i in range(n_chains)]
    v_s = [v_scr[i] for i in range(n_chains)]
    gam_rows = [gam_scr[i][0:1] for i in range(n_chains)]
    gram = [_bdot_nt(a, b) for a, b in zip(x1, x2)]
    akk_m = [jnp.where(m, twice(g[:L, :n]), 0.0) for m, g in zip(strict, gram)]
    akb_m = [jnp.where(m, twice(g[:L, n:]), 0.0) for m, g in zip(strict, gram)]
    ark_m = [jnp.where(m, twice(g[L:, :n]), 0.0) for m, g in zip(incl, gram)]
    arb_m = [jnp.where(m, twice(g[L:, n:]), 0.0) for m, g in zip(incl, gram)]

    t_inv = []
    inverse = _inverse_stages(akb_m, eye, blks, odds, t_inv)
    av = [None] * n_chains

    def apply_values(lo, hi):
        for i in range(lo, hi):
            av[i] = _bdot(jnp.concatenate([akk_m[i], ark_m[i]], axis=0), v_s[i])

    next(inverse, None)
    next(inverse, None)

    taps = taps_ref[...]
    row8 = lax.broadcasted_iota(jnp.int32, (SUBLANES, 1), 0)

    def shifted(m_ref, p_ref, n_ref, block):
        main = m_ref[0]
        prev_row = jnp.where(block > 0, p_ref[0][SUBLANES - 1:SUBLANES], 0.0)
        next_row = jnp.where(block < n_blocks - 1, n_ref[0][0:1], 0.0)
        up = pltpu.roll(main, 1, 0)
        dn = pltpu.roll(main, SL - 1, 0)
        up = jnp.concatenate([jnp.where(row8 == 0, prev_row, up[:SUBLANES]), up[SUBLANES:]], axis=0)
        dn = jnp.concatenate([dn[:SL - SUBLANES],
                              jnp.where(row8 == SUBLANES - 1, next_row, dn[SL - SUBLANES:])], axis=0)
        return taps[0:1] * up + taps[1:2] * main + taps[2:3] * dn

    cf = s
    cb = n_blocks - 1 - s
    sh = (shifted(fm_ref, fp_ref, fn_ref, cf), shifted(bm_ref, bp_ref, bn_ref, cb))
    r = [x[:, 0:W] for x in sh]
    k = [x[:, W:2 * W] for x in sh]
    v = [x[:, 2 * W:3 * W] for x in sh]
    wd = [x[:, 3 * W:3 * W + LORA_RANK] for x in sh]
    ad = [x[:, 3 * W + LORA_RANK:3 * W + 2 * LORA_RANK] for x in sh]
    lora_w = [_bdot(jnp.tanh(wd[z]), wup_ref[z]) for z in range(N_DIRS)]
    lora_a = [_bdot(ad[z], aup_ref[z]) for z in range(N_DIRS)]
    lora_a0b = _bdot(ad[1], aup_ref[0])
    kkr = [x * kk_ref[...] for x in k]
    kk_ss = [_bdot(x * x, ones_ref[...]) for x in kkr]

    next(inverse, None)
    next(inverse, None)

    lw = [-DECAY_SCALE * _sigmoid(w0_ref[z] + lora_w[z]) for z in range(N_DIRS)]
    a = [_sigmoid(a0_ref[z] + lora_a[z]) for z in range(N_DIRS)]
    kk = [x * lax.rsqrt(y + L2_EPS) for x, y in zip(kkr, kk_ss)]
    kt = [k[z] * (1.0 + (a[z] - 1.0) * ka_ref[...]) for z in range(N_DIRS)]
    akk = [a[z] * kk[z] for z in range(N_DIRS)]
    lw_hi = [x.astype(BF16) for x in lw]
    lw_lo = [(x - y.astype(F32)).astype(BF16) for x, y in zip(lw, lw_hi)]
    g = [jnp.dot(tri_ref[z], lw_hi[z], preferred_element_type=F32)
         + jnp.dot(tri_ref[z], lw_lo[z], preferred_element_type=F32) for z in range(N_DIRS)]
    a0_b = _sigmoid(a0_ref[0] + lora_a0b)
    kt0_b = k[1] * (1.0 + (a0_b - 1.0) * ka_ref[...])
    rk_sum = r[1] * (kt0_b + kt[1]) * rk_ref[...]
    rk_hi = rk_sum.astype(BF16)
    rk_lo = (rk_sum - rk_hi.astype(F32)).astype(BF16)
    bonus_dots = (jnp.dot(rk_hi, ones_ref[...], preferred_element_type=F32)
                  + jnp.dot(rk_lo, ones_ref[...], preferred_element_type=F32))

    quarter = n_chains // 4
    for part in range(4):
        next(inverse, None)
        apply_values(part * quarter, (part + 1) * quarter)
    next(inverse, None)
    assert len(t_inv) == n_chains
    wu = [_bdot(t, jnp.concatenate([twice(p[:L]), q[:n].astype(BF16)], axis=1))
          for t, p, q in zip(t_inv, x1, av)]
    aw = [_bdot(p, q) for p, q in zip(arb_m, wu)]
    qe = [twice(p[L:]).astype(F32) - q[:, :PAIR] for p, q in zip(x1, aw)]
    yv = [p[n:] - q[:, PAIR:] for p, q in zip(av, aw)]
    c0 = [_bdot_tn(p, jnp.concatenate([q, (-u[:, PAIR:]).astype(BF16)], axis=0))
          for p, q, u in zip(x3, v_s, wu)]
    bw = [_bdot_tn(p[n:], u[:, :PAIR]) for p, u in zip(x3, wu)]
    m_c = [jnp.where(eye, jnp.broadcast_to(g, (n, PAIR)), 0.0) - jnp.where(blk64, q, 0.0)
           for g, q in zip(gam_rows, bw)]

    lanes = [(z, p) for z in range(N_DIRS) for p in range(N_PAIRS)]
    h = [jnp.where(chains_start_sequence, 0.0, h_scr[z * N_PAIRS + p]) for z, p in lanes]
    ys = [None] * n_chains
    y_lane_lo = lax.broadcasted_iota(jnp.int32, (L, PAIR), 1) < HEAD_DIM
    for step in range(cps):
        idx = [chains.index((z, step if z == 0 else cps - 1 - step, p)) for z, p in lanes]
        y_s = [_bdot(qe[i], hh) + yv[i] for i, hh in zip(idx, h)]
        h = [_bdot(m_c[i], hh) + c0[i] for i, hh in zip(idx, h)]
        for i, y in zip(idx, y_s):
            ys[i] = jnp.where(y_lane_lo, y[:L], y[L:])
    for (z, p), hh in zip(lanes, h):
        h_scr[z * N_PAIRS + p] = hh

    def assemble(z):
        return jnp.concatenate(
            [jnp.concatenate([ys[chains.index((z, j, p))] for p in range(N_PAIRS)], axis=1)
             for j in range(cps)], axis=0)

    yf_ref[0] = assemble(0).astype(BF16)
    yb_ref[0] = assemble(1).astype(BF16)

    bon_ref[0] = (bonus_dots * v[1]).astype(BF16)
    lane_lo = lax.broadcasted_iota(jnp.int32, (L, PAIR), 1) < HEAD_DIM
    zero = jnp.zeros((L, PAIR), BF16)

    def stack(x, j, p):
        xp = x[j * L:(j + 1) * L, p * PAIR:(p + 1) * PAIR]
        return jnp.concatenate([jnp.where(lane_lo, xp, zero), jnp.where(lane_lo, zero, xp)], axis=0)

    for z in range(N_DIRS):
        g_tot = [jnp.sum(lw[z][j * L:(j + 1) * L], axis=0, keepdims=True) for j in range(cps)]
        g_tot_rows = jnp.concatenate([jnp.broadcast_to(x, (L, W)) for x in g_tot], axis=0)
        e_out = jnp.exp(-g[z])
        e_rem = jnp.exp(g_tot_rows - g[z])
        kkg = (kk[z] * jnp.exp(g[z] - lw[z])).astype(BF16)
        rg = (r[z] * jnp.exp(g[z])).astype(BF16)
        kd = (kt[z] * e_out).astype(BF16)
        bd = (akk[z] * e_out).astype(BF16)
        kdg = (kt[z] * e_rem).astype(BF16)
        bdg = (akk[z] * e_rem).astype(BF16)
        v_bf = v[z].astype(BF16)
        for j in range(cps):
            gam = jnp.broadcast_to(jnp.exp(g_tot[j]), (SUBLANES, W))
            for p in range(N_PAIRS):
                i = chains.index((z, j, p))
                rows, cols = slice(j * L, (j + 1) * L), slice(p * PAIR, (p + 1) * PAIR)
                x1_scr[i] = jnp.concatenate([kkg[rows, cols], rg[rows, cols]], axis=0)
                x2_scr[i] = jnp.concatenate([stack(kd, j, p), stack(bd, j, p)], axis=0)
                x3_scr[i] = jnp.concatenate([stack(kdg, j, p), stack(bdg, j, p)], axis=0)
                v_scr[i] = stack(v_bf, j, p)
                gam_scr[i] = gam[:, p * PAIR:(p + 1) * PAIR]


def _rwkv_call(rin, taps, tri, w_up, w0, a_up, a0, k_k, k_a, r_k, ones):
    B, T, C = rin.shape
    L = CHUNK * CHUNKS_PER_STEP
    nC = T // L
    hb = L // SUBLANES
    n_hb = T // SUBLANES
    W = RWKV_WIDTH
    n_chains = N_DIRS * CHUNKS_PER_STEP * N_PAIRS

    n_total = B * nC
    def prep(g): return jnp.minimum(g, n_total - 1)
    def chain(g): return jnp.maximum(g - 1, 0)
    def seq(blk): return blk // nC
    def pos_f(blk): return blk % nC
    def pos_b(blk): return nC - 1 - blk % nC
    def main_f(g): return (seq(prep(g)), pos_f(prep(g)), 0)
    def prev_f(g): return (seq(prep(g)), jnp.maximum(pos_f(prep(g)) * hb - 1, 0), 0)
    def next_f(g): return (seq(prep(g)), jnp.minimum((pos_f(prep(g)) + 1) * hb, n_hb - 1), 0)
    def main_b(g): return (seq(prep(g)), pos_b(prep(g)), 0)
    def prev_b(g): return (seq(prep(g)), jnp.maximum(pos_b(prep(g)) * hb - 1, 0), 0)
    def next_b(g): return (seq(prep(g)), jnp.minimum((pos_b(prep(g)) + 1) * hb, n_hb - 1), 0)
    def out_f(g): return (seq(chain(g)), pos_f(chain(g)), 0)
    def out_b(g): return (seq(chain(g)), pos_b(chain(g)), 0)

    const2 = lambda g: (0, 0)
    const3 = lambda g: (0, 0, 0)
    out_shape = tuple(jax.ShapeDtypeStruct((B, T, W), BF16) for _ in range(3))
    n = 2 * CHUNK
    return pl.pallas_call(
        functools.partial(_rwkv_kernel, n_blocks=nC),
        name="rwkv7_chunked",
        grid=(n_total + 1,),
        in_specs=[pl.BlockSpec((1, L, C), main_f),
                  pl.BlockSpec((1, SUBLANES, C), prev_f),
                  pl.BlockSpec((1, SUBLANES, C), next_f),
                  pl.BlockSpec((1, L, C), main_b),
                  pl.BlockSpec((1, SUBLANES, C), prev_b),
                  pl.BlockSpec((1, SUBLANES, C), next_b),
                  pl.BlockSpec((3, C), const2),
                  pl.BlockSpec((N_DIRS, L, L), const3),
                  pl.BlockSpec((N_DIRS, LORA_RANK, W), const3),
                  pl.BlockSpec((N_DIRS, 1, W), const3),
                  pl.BlockSpec((N_DIRS, LORA_RANK, W), const3),
                  pl.BlockSpec((N_DIRS, 1, W), const3),
                  pl.BlockSpec((1, W), const2),
                  pl.BlockSpec((1, W), const2),
                  pl.BlockSpec((1, W), const2),
                  pl.BlockSpec((W, W), const2)],
        out_specs=(pl.BlockSpec((1, L, W), out_f),
                   pl.BlockSpec((1, L, W), out_b),
                   pl.BlockSpec((1, L, W), main_b)),
        out_shape=out_shape,
        scratch_shapes=[pltpu.VMEM((N_DIRS * N_PAIRS, n, PAIR), F32),
                        pltpu.VMEM((n_chains, n, PAIR), BF16),
                        pltpu.VMEM((n_chains, 2 * n, PAIR), BF16),
                        pltpu.VMEM((n_chains, 2 * n, PAIR), BF16),
                        pltpu.VMEM((n_chains, n, PAIR), BF16),
                        pltpu.VMEM((n_chains, SUBLANES, PAIR), F32)],
        compiler_params=pltpu.CompilerParams(dimension_semantics=("arbitrary",),
                                             vmem_limit_bytes=VMEM_LIMIT),
    )(rin, rin, rin, rin, rin, rin, taps, tri, w_up, w0, a_up, a0, k_k, k_a, r_k, ones)


def _outproj_kernel(att_ref, yf_ref, yb_ref, bon_ref, grw_ref, x_ref, mod_ref, w_ref,
                    gnw_ref, gnb_ref, gpost_ref, ones_ref, o_ref):
    out_att = jnp.dot(att_ref[0], w_ref[:ATT_WIDTH, :], preferred_element_type=F32)
    y = yf_ref[0].astype(F32) + yb_ref[0].astype(F32)
    ones = ones_ref[...]
    inv_n = 1.0 / HEAD_DIM
    mu = _bdot(y, ones) * inv_n
    d = y - mu
    var = _bdot(d * d, ones) * inv_n
    yn = d * lax.rsqrt(var + GN_EPS) * gnw_ref[...] + gnb_ref[...]
    g = grw_ref[0].astype(F32)
    rw = ((yn + bon_ref[0].astype(F32)) * (g * _sigmoid(g))).astype(BF16)
    out = out_att + jnp.dot(rw, w_ref[ATT_WIDTH:, :], preferred_element_type=F32)
    ms = jnp.mean(out * out, axis=-1, keepdims=True)
    on = out * lax.rsqrt(ms + NORM_EPS) * gpost_ref[...]
    o_ref[0] = x_ref[0] + mod_ref[0] * on


def _outproj_call(att, yf, yb, bon, grw, x, mod3, w_out_bf, gn_w, gn_b, g_post, ones, tm):
    B, T, D = x.shape
    W = RWKV_WIDTH
    row = lambda t, b: (b, t, 0)
    const2 = lambda t, b: (0, 0)
    return pl.pallas_call(
        _outproj_kernel,
        name="out_proj",
        grid=(T // tm, B),
        in_specs=[pl.BlockSpec((1, tm, ATT_WIDTH), row),
                  pl.BlockSpec((1, tm, W), row),
                  pl.BlockSpec((1, tm, W), row),
                  pl.BlockSpec((1, tm, W), row),
                  pl.BlockSpec((1, tm, W), row),
                  pl.BlockSpec((1, tm, D), row),
                  pl.BlockSpec((1, 1, D), lambda t, b: (b, 0, 2)),
                  pl.BlockSpec((ATT_WIDTH + W, D), const2),
                  pl.BlockSpec((1, W), const2),
                  pl.BlockSpec((1, W), const2),
                  pl.BlockSpec((1, D), const2),
                  pl.BlockSpec((W, W), const2)],
        out_specs=pl.BlockSpec((1, tm, D), row),
        out_shape=jax.ShapeDtypeStruct((B, T, D), F32),
        compiler_params=pltpu.CompilerParams(dimension_semantics=("arbitrary", "arbitrary"),
                                             vmem_limit_bytes=VMEM_LIMIT),
    )(att, yf, yb, bon, grw, x, mod3, w_out_bf, gn_w, gn_b, g_post, ones)


def _pick_tile(T, target):
    t = min(T, target)
    while T % t:
        t //= 2
    return t


def _permute_heads(w, axis):
    blocks = jnp.split(w, ATT_Q_HEADS, axis=axis)
    return jnp.concatenate([blocks[h] for h in Q_HEAD_ORDER], axis=axis)


def kernel(x, c, w_ada, b_ada, g_pre, w_in, q_norm_g, k_norm_g, shift_taps, w_up, w0, a_up, a0,
           k_k, k_a, r_k, gn_w, gn_b, w_out, g_post):
    B, T, D = x.shape
    depth = w_ada.shape[0]
    assert T % (CHUNK * CHUNKS_PER_STEP) == 0 and T % GRID_W == 0
    tm_in = _pick_tile(T, IN_PROJ_ROWS)
    tm_out = _pick_tile(T, OUT_PROJ_ROWS)
    tq = _pick_tile(T, ATTN_QUERY_ROWS)
    cos, sin = _rope_tables(T)
    ones = _block_ones(RWKV_WIDTH)
    ti = jnp.arange(CHUNK * CHUNKS_PER_STEP)
    same_chunk = (ti[None, :] // CHUNK) == (ti[:, None] // CHUNK)
    tri = jnp.stack([same_chunk & (ti[None, :] <= ti[:, None]),
                     same_chunk & (ti[None, :] >= ti[:, None])]).astype(BF16)
    for l in range(depth):
        mod = _ada_call(c, w_ada[l], b_ada[l])
        mod3 = mod.reshape(B, 1, 3 * D)
        qg = jnp.tile(q_norm_g[l], ATT_Q_HEADS).reshape(1, ATT_WIDTH)
        kg = jnp.tile(k_norm_g[l], ATT_KV_HEADS).reshape(1, ATT_KV_WIDTH)
        wq = _permute_heads(w_in[l][:, Q_OFF:K_OFF], 1).astype(BF16)
        q, k, vt, g_att, rin, g_rw = _inproj_call(
            x, mod3, g_pre[l].reshape(1, D), w_in[l].astype(BF16), wq, qg, kg, cos, sin, ones, tm_in)
        att = _attn_call(q, k, vt, g_att, tq)
        yf, yb, bon = _rwkv_call(
            rin, shift_taps[l], tri, w_up[l].astype(BF16), w0[l].reshape(N_DIRS, 1, RWKV_WIDTH),
            a_up[l].astype(BF16), a0[l].reshape(N_DIRS, 1, RWKV_WIDTH),
            k_k[l].reshape(1, RWKV_WIDTH), k_a[l].reshape(1, RWKV_WIDTH),
            r_k[l].reshape(1, RWKV_WIDTH), ones)
        x = _outproj_call(att, yf, yb, bon, g_rw, x, mod3, w_out[l].astype(BF16),
                          gn_w[l].reshape(1, RWKV_WIDTH), gn_b[l].reshape(1, RWKV_WIDTH),
                          g_post[l].reshape(1, D), ones, tm_out)
    return x
```

```python
import functools

import jax
import jax.numpy as jnp
from jax import lax
from jax.experimental import pallas as pl
from jax.experimental.pallas import tpu as pltpu

F32 = jnp.float32
BF16 = jnp.bfloat16

HEAD_DIM = 64
ATT_Q_HEADS = 8
ATT_KV_HEADS = 2
ATT_GROUP = ATT_Q_HEADS // ATT_KV_HEADS
ATT_WIDTH = ATT_Q_HEADS * HEAD_DIM
ATT_KV_WIDTH = ATT_KV_HEADS * HEAD_DIM
RWKV_HEADS = 8
RWKV_WIDTH = RWKV_HEADS * HEAD_DIM
LORA_RANK = 64
RWKV_SHIFT_WIDTH = 3 * RWKV_WIDTH + 2 * LORA_RANK
N_DIRS = 2
GRID_W = 64
ROPE_THETA = 10000.0
DECAY_SCALE = 0.6065306597126334
NORM_EPS = 1e-6
GN_EPS = 64e-5
L2_EPS = 1e-12

Q_OFF = 0
K_OFF = Q_OFF + ATT_WIDTH
V_OFF = K_OFF + ATT_KV_WIDTH
GATT_OFF = V_OFF + ATT_KV_WIDTH
RIN_OFF = GATT_OFF + ATT_WIDTH
GRW_OFF = RIN_OFF + RWKV_SHIFT_WIDTH
IN_WIDTH = GRW_OFF + RWKV_WIDTH

LANES = 128
SUBLANES = 8
CHUNK = 64
PAIR = 2 * HEAD_DIM
BF16_SUBLANES = 16
VT_ONES = BF16_SUBLANES
VT_KV = HEAD_DIM + VT_ONES
VT_ROWS = ATT_KV_HEADS * VT_KV
Q_SCALE = HEAD_DIM ** -0.5 * 1.4426950408889634
Q_HEAD_ORDER = tuple(g + kv * ATT_GROUP for g in range(ATT_GROUP) for kv in range(ATT_KV_HEADS))
VMEM_LIMIT = 56 * 1024 * 1024
IN_PROJ_ROWS = 256
OUT_PROJ_ROWS = 512
ATTN_QUERY_ROWS = 512
ATTN_KEY_CHUNK = 256
ATTN_SCORES_AHEAD = 1
ATTN_MAX_INIT = -1e30


def _bdot(a, b):
    return jnp.dot(a.astype(BF16), b.astype(BF16), preferred_element_type=F32)


def _bdot_nt(a, b):
    return lax.dot_general(a.astype(BF16), b.astype(BF16), (((1,), (1,)), ((), ())),
                           preferred_element_type=F32)


def _bdot_tn(a, b):
    return lax.dot_general(a.astype(BF16), b.astype(BF16), (((0,), (0,)), ((), ())),
                           preferred_element_type=F32)


def _sigmoid(x):
    return 1.0 / (1.0 + jnp.exp(-x))


def _ada_kernel(c_ref, w_ref, b_ref, o_ref):
    c = c_ref[...]
    ca = c * _sigmoid(c)
    o_ref[...] = jnp.dot(ca, w_ref[...], precision=lax.Precision.HIGHEST,
                         preferred_element_type=F32) + b_ref[...]


def _ada_call(c, w_ada, b_ada):
    B, D = c.shape
    n_out = w_ada.shape[1]
    tn = D
    return pl.pallas_call(
        _ada_kernel,
        name="ada_mod",
        grid=(n_out // tn,),
        in_specs=[pl.BlockSpec((B, D), lambda j: (0, 0)),
                  pl.BlockSpec((D, tn), lambda j: (0, j)),
                  pl.BlockSpec((1, tn), lambda j: (0, j))],
        out_specs=pl.BlockSpec((B, tn), lambda j: (0, j)),
        out_shape=jax.ShapeDtypeStruct((B, n_out), F32),
        compiler_params=pltpu.CompilerParams(dimension_semantics=("arbitrary",),
                                             vmem_limit_bytes=VMEM_LIMIT),
    )(c, w_ada, b_ada.reshape(1, n_out))


def _rope(x, cos, sin_signed, first_half):
    n = x.shape[1]
    partner = jnp.where(first_half, pltpu.roll(x, n - 16, 1), pltpu.roll(x, 16, 1))
    return x * cos + partner * sin_signed


def _inproj_kernel(x_ref, mod_ref, gpre_ref, w_ref, wq_ref, qg_ref, kg_ref, cos_ref, sin_ref, ones_ref,
                   q_ref, k_ref, vt_ref, gatt_ref, rin_ref, grw_ref):
    D = x_ref.shape[2]
    x = x_ref[0]
    ms = jnp.mean(x * x, axis=-1, keepdims=True)
    xn = x * lax.rsqrt(ms + NORM_EPS) * gpre_ref[...]
    mod = mod_ref[0]
    shift = mod[:, :D]
    scale = mod[:, D:2 * D]
    h = (xn * (1.0 + scale) + shift).astype(BF16)

    def proj(lo, width):
        return jnp.dot(h, w_ref[:, lo:lo + width], preferred_element_type=F32)

    cos = cos_ref[...]
    sin = sin_ref[...]
    lane = lax.broadcasted_iota(jnp.int32, cos.shape, 1)
    first_half = (lane % 32) < 16

    def mean_sq(y):
        w = y.shape[1]
        return _bdot(y * y, ones_ref[:w, :w]) * (1.0 / HEAD_DIM)

    q = jnp.dot(h, wq_ref[...], preferred_element_type=F32)
    k = proj(K_OFF, ATT_KV_WIDTH)
    v = proj(V_OFF, ATT_KV_WIDTH).astype(BF16)
    rin_ref[0] = proj(RIN_OFF, RWKV_SHIFT_WIDTH)
    q_msq = mean_sq(q)
    k_msq = mean_sq(k)
    eye = (lax.broadcasted_iota(jnp.int32, (ATT_KV_WIDTH, ATT_KV_WIDTH), 0)
           == lax.broadcasted_iota(jnp.int32, (ATT_KV_WIDTH, ATT_KV_WIDTH), 1)).astype(BF16)
    vt = _bdot_nt(eye, v).astype(BF16)
    gatt_ref[0] = proj(GATT_OFF, ATT_WIDTH).astype(BF16)
    grw_ref[0] = proj(GRW_OFF, RWKV_WIDTH).astype(BF16)

    q = q * lax.rsqrt(q_msq + NORM_EPS) * qg_ref[...]
    reps = ATT_WIDTH // LANES
    cos_q = jnp.concatenate([cos] * reps, axis=1)
    sin_q = jnp.concatenate([sin] * reps, axis=1)
    lane_q = lax.broadcasted_iota(jnp.int32, cos_q.shape, 1)
    fh_q = (lane_q % 32) < 16
    q_ref[0] = (_rope(q, cos_q, sin_q, fh_q) * Q_SCALE).astype(BF16)
    k = k * lax.rsqrt(k_msq + NORM_EPS) * kg_ref[...]
    k_ref[0] = _rope(k, cos, sin, first_half).astype(BF16)
    ones_rows = jnp.ones((VT_ONES, vt.shape[1]), BF16)
    vt_ref[0] = jnp.concatenate([vt[:HEAD_DIM], ones_rows, vt[HEAD_DIM:], ones_rows], axis=0)


def _rope_tables(T):
    rows = T // GRID_W
    row = jnp.repeat(jnp.arange(rows, dtype=F32), GRID_W)
    col = jnp.tile(jnp.arange(GRID_W, dtype=F32), rows)
    n_freq = HEAD_DIM // 4
    inv_freq = ROPE_THETA ** (-jnp.arange(n_freq, dtype=F32) / n_freq)
    ang_r = row[:, None] * inv_freq
    ang_c = col[:, None] * inv_freq
    cr, sr, cc, sc = jnp.cos(ang_r), jnp.sin(ang_r), jnp.cos(ang_c), jnp.sin(ang_c)
    cos = jnp.concatenate([cr, cr, cc, cc], axis=1)
    sin = jnp.concatenate([-sr, sr, -sc, sc], axis=1)
    reps = LANES // HEAD_DIM
    return jnp.tile(cos, (1, reps)), jnp.tile(sin, (1, reps))


def _block_ones(n):
    i = jnp.arange(n) // HEAD_DIM
    return (i[:, None] == i[None, :]).astype(BF16)


def _inproj_call(x, mod3, g_pre, w_in_bf, wq_bf, qg, kg, cos, sin, ones, tm):
    B, T, D = x.shape
    grid = (T // tm, B)
    row = lambda t, b: (b, t, 0)
    const2 = lambda t, b: (0, 0)
    out_shapes = (
        jax.ShapeDtypeStruct((B, T, ATT_WIDTH), BF16),
        jax.ShapeDtypeStruct((B, T, ATT_KV_WIDTH), BF16),
        jax.ShapeDtypeStruct((B, VT_ROWS, T), BF16),
        jax.ShapeDtypeStruct((B, T, ATT_WIDTH), BF16),
        jax.ShapeDtypeStruct((B, T, RWKV_SHIFT_WIDTH), F32),
        jax.ShapeDtypeStruct((B, T, RWKV_WIDTH), BF16),
    )
    out_specs = tuple(
        pl.BlockSpec((1, VT_ROWS, tm), lambda t, b: (b, 0, t)) if i == 2
        else pl.BlockSpec((1, tm, s.shape[2]), row) for i, s in enumerate(out_shapes))
    return pl.pallas_call(
        _inproj_kernel,
        name="in_proj",
        grid=grid,
        in_specs=[pl.BlockSpec((1, tm, D), row),
                  pl.BlockSpec((1, 1, mod3.shape[2]), lambda t, b: (b, 0, 0)),
                  pl.BlockSpec((1, D), const2),
                  pl.BlockSpec((D, IN_WIDTH), const2),
                  pl.BlockSpec((D, ATT_WIDTH), const2),
                  pl.BlockSpec((1, ATT_WIDTH), const2),
                  pl.BlockSpec((1, ATT_KV_WIDTH), const2),
                  pl.BlockSpec((tm, LANES), lambda t, b: (t, 0)),
                  pl.BlockSpec((tm, LANES), lambda t, b: (t, 0)),
                  pl.BlockSpec((ATT_WIDTH, ATT_WIDTH), const2)],
        out_specs=out_specs,
        out_shape=out_shapes,
        compiler_params=pltpu.CompilerParams(dimension_semantics=("arbitrary", "arbitrary"),
                                             vmem_limit_bytes=VMEM_LIMIT),
    )(x, mod3, g_pre, w_in_bf, wq_bf, qg, kg, cos, sin, ones)


def _attn_kernel(q_ref, k_ref, vt_ref, g_ref, o_ref):
    q = q_ref[0]
    k = k_ref[0]
    tq = q.shape[0]
    lane_lo = lax.broadcasted_iota(jnp.int32, (tq, PAIR), 1) < HEAD_DIM
    zero = jnp.zeros((tq, PAIR), BF16)
    T = k.shape[0]
    kc = min(T, ATTN_KEY_CHUNK)
    n_kc = T // kc
    qs = []
    for j in range(ATT_GROUP):
        qp = q[:, j * PAIR:(j + 1) * PAIR]
        qs.append(jnp.concatenate([jnp.where(lane_lo, qp, zero), jnp.where(lane_lo, zero, qp)], axis=0))

    def scores(item):
        j, c = item
        return lax.dot_general(k[c * kc:(c + 1) * kc], qs[j], (((1,), (1,)), ((), ())),
                               preferred_element_type=F32)

    items = [(j, c) for j in range(ATT_GROUP) for c in range(n_kc)]
    outs = [None] * ATT_Q_HEADS
    pending = [scores(item) for item in items[:ATTN_SCORES_AHEAD]]
    for idx, (j, c) in enumerate(items):
        s = pending.pop(0)
        if idx + ATTN_SCORES_AHEAD < len(items):
            pending.append(scores(items[idx + ATTN_SCORES_AHEAD]))
        if c == 0:
            m = jnp.full((1, 2 * tq), ATTN_MAX_INIT, F32)
            acc = [jnp.zeros((VT_KV, tq), F32) for _ in range(ATT_KV_HEADS)]
        m_new = jnp.maximum(m, jnp.max(s, axis=0, keepdims=True))
        alpha = jnp.exp2(m - m_new)
        p = jnp.exp2(s - m_new).astype(BF16)
        m = m_new
        for kv in range(ATT_KV_HEADS):
            vt = vt_ref[0, kv * VT_KV:(kv + 1) * VT_KV, c * kc:(c + 1) * kc]
            acc[kv] = (acc[kv] * alpha[:, kv * tq:(kv + 1) * tq]
                       + jnp.dot(vt, p[:, kv * tq:(kv + 1) * tq], preferred_element_type=F32))
        if c == n_kc - 1:
            for kv in range(ATT_KV_HEADS):
                o = acc[kv]
                outs[Q_HEAD_ORDER[j * ATT_KV_HEADS + kv]] = o[:HEAD_DIM] / o[HEAD_DIM:HEAD_DIM + 1]
    y = jnp.concatenate(outs, axis=0).T
    g = g_ref[0].astype(F32)
    o_ref[0] = (y * (g * _sigmoid(g))).astype(BF16)


def _attn_call(q, k, vt, g_att, tq):
    B, T, _ = q.shape
    return pl.pallas_call(
        _attn_kernel,
        name="gqa_attn",
        grid=(B, T // tq),
        in_specs=[pl.BlockSpec((1, tq, ATT_WIDTH), lambda b, i: (b, i, 0)),
                  pl.BlockSpec((1, T, ATT_KV_WIDTH), lambda b, i: (b, 0, 0)),
                  pl.BlockSpec((1, VT_ROWS, T), lambda b, i: (b, 0, 0)),
                  pl.BlockSpec((1, tq, ATT_WIDTH), lambda b, i: (b, i, 0))],
        out_specs=pl.BlockSpec((1, tq, ATT_WIDTH), lambda b, i: (b, i, 0)),
        out_shape=jax.ShapeDtypeStruct((B, T, ATT_WIDTH), BF16),
        compiler_params=pltpu.CompilerParams(dimension_semantics=("arbitrary", "arbitrary"),
                                             vmem_limit_bytes=VMEM_LIMIT),
    )(q, k, vt, g_att)


CHUNKS_PER_STEP = 4
N_PAIRS = RWKV_WIDTH // PAIR


def _later_rows(x, s, odd):
    first = s if odd else 0
    return jnp.concatenate([x[b:b + s] for b in range(first, x.shape[0], 2 * s)], axis=0)


def _merge_rows(keep, new, s, odd):
    out = []
    for i, b in enumerate(range(0, keep.shape[0], 2 * s)):
        lo, hi = keep[b:b + s], keep[b + s:b + 2 * s]
        blk = new[i * s:(i + 1) * s]
        out += [lo, blk] if odd else [blk, hi]
    return jnp.concatenate(out, axis=0)


INVERSE_STAGES = 9


def _inverse_stages(a_list, eye, blks, odds, out):
    blk8, blk16, blk32, blk64 = blks
    a8 = [jnp.where(blk8, a, 0.0) for a in a_list]
    a2 = [_bdot(x, x) for x in a8]
    yield
    a4 = [_bdot(x, x) for x in a2]
    t = [_bdot(eye - x, eye + y) for x, y in zip(a8, a2)]
    yield
    t = [_bdot(x, eye + y) for x, y in zip(t, a4)]
    for inner, outer, s in ((blk8, blk16, 8), (blk16, blk32, 16), (blk32, blk64, 32)):
        yield
        off = jnp.logical_and(outer, jnp.logical_not(inner))
        xs = [_bdot(_later_rows(x, s, o), jnp.where(off, a, 0.0)) for x, a, o in zip(t, a_list, odds)]
        yield
        upd = [_later_rows(x, s, o) - _bdot(y, x) for x, y, o in zip(t, xs, odds)]
        t = [_merge_rows(x, u, s, o) for x, u, o in zip(t, upd, odds)]
    out.extend(t)


def _rwkv_kernel(fm_ref, fp_ref, fn_ref, bm_ref, bp_ref, bn_ref, taps_ref, tri_ref,
                 wup_ref, w0_ref, aup_ref, a0_ref, kk_ref, ka_ref, rk_ref, ones_ref,
                 yf_ref, yb_ref, bon_ref,
                 h_scr, x1_scr, x2_scr, x3_scr, v_scr, gam_scr, *, n_blocks):
    g_step = pl.program_id(0)
    last_block = pl.num_programs(0) - 2
    s = lax.rem(jnp.minimum(g_step, last_block), jnp.int32(n_blocks))
    chains_start_sequence = lax.rem(jnp.maximum(g_step - 1, 0), jnp.int32(n_blocks)) == 0

    @pl.when(g_step == 0)
    def _():
        def zero_chain(i, carry):
            for ref in (x1_scr, x2_scr, x3_scr, v_scr, gam_scr):
                ref[i] = jnp.zeros(ref.shape[1:], ref.dtype)
            return carry

        def zero_state(i, carry):
            h_scr[i] = jnp.zeros(h_scr.shape[1:], h_scr.dtype)
            return carry

        lax.fori_loop(0, x1_scr.shape[0], zero_chain, 0)
        lax.fori_loop(0, h_scr.shape[0], zero_state, 0)

    SL = fm_ref.shape[1]
    L = CHUNK
    cps = SL // L
    W = RWKV_WIDTH
    n = 2 * L
    n_chains = x1_scr.shape[0]

    ri = lax.broadcasted_iota(jnp.int32, (n, n), 0)
    ci = lax.broadcasted_iota(jnp.int32, (n, n), 1)
    eye = ri == ci
    blk = lambda size: (ri // size) == (ci // size)
    blk8, blk16, blk32, blk64 = blk(8), blk(16), blk(32), blk(64)
    stricts = (jnp.logical_and(blk64, ci < ri), jnp.logical_and(blk64, ci > ri))
    chains = [(z, j, p) for z in range(N_DIRS) for j in range(cps) for p in range(N_PAIRS)]
    assert len(chains) == n_chains
    strict = [stricts[z] for z, _, _ in chains]
    incl = [jnp.logical_or(m, eye) for m in strict]
    odds = [z == 0 for z, _, _ in chains]
    blks = (blk8, blk16, blk32, blk64)

    x1 = [x1_scr[i] for i in range(n_chains)]
    twice = lambda x: jnp.concatenate([x, x], axis=0)
    x2 = [x2_scr[i] for i in range(n_chains)]
    x3 = [x3_scr[i] for i in range(n_chains)]
    v_s = [v_scr[i] for i in range(n_chains)]
    gam_rows = [gam_scr[i][0:1] for i in range(n_chains)]
    gram = [_bdot_nt(a, b) for a, b in zip(x1, x2)]
    akb_m = [jnp.where(m, twice(g[:L, n:]), 0.0) for m, g in zip(strict, gram)]
    arb_m = [jnp.where(m, twice(g[L:, n:]), 0.0) for m, g in zip(incl, gram)]
    rt = lax.broadcasted_iota(jnp.int32, (L, n), 0)
    ct = lax.broadcasted_iota(jnp.int32, (L, n), 1) % L
    strict_t = (ct < rt, ct > rt)
    incl_t = tuple(jnp.logical_or(m, ct == rt) for m in strict_t)

    t_inv = []
    inverse = _inverse_stages(akb_m, eye, blks, odds, t_inv)
    av = [None] * n_chains

    def apply_values(lo, hi):
        for i in range(lo, hi):
            z = chains[i][0]
            a_kd = jnp.concatenate([jnp.where(strict_t[z], gram[i][:L, :n], 0.0),
                                    jnp.where(incl_t[z], gram[i][L:, :n], 0.0)], axis=0)
            av[i] = _bdot(a_kd, v_s[i])

    next(inverse, None)
    next(inverse, None)

    taps = taps_ref[...]
    row8 = lax.broadcasted_iota(jnp.int32, (SUBLANES, 1), 0)

    def shifted(m_ref, p_ref, n_ref, block):
        main = m_ref[0]
        prev_row = jnp.where(block > 0, p_ref[0][SUBLANES - 1:SUBLANES], 0.0)
        next_row = jnp.where(block < n_blocks - 1, n_ref[0][0:1], 0.0)
        up = pltpu.roll(main, 1, 0)
        dn = pltpu.roll(main, SL - 1, 0)
        up = jnp.concatenate([jnp.where(row8 == 0, prev_row, up[:SUBLANES]), up[SUBLANES:]], axis=0)
        dn = jnp.concatenate([dn[:SL - SUBLANES],
                              jnp.where(row8 == SUBLANES - 1, next_row, dn[SL - SUBLANES:])], axis=0)
        return taps[0:1] * up + taps[1:2] * main + taps[2:3] * dn

    cf = s
    cb = n_blocks - 1 - s
    sh = (shifted(fm_ref, fp_ref, fn_ref, cf), shifted(bm_ref, bp_ref, bn_ref, cb))
    r = [x[:, 0:W] for x in sh]
    k = [x[:, W:2 * W] for x in sh]
    v = [x[:, 2 * W:3 * W] for x in sh]
    wd = [x[:, 3 * W:3 * W + LORA_RANK] for x in sh]
    ad = [x[:, 3 * W + LORA_RANK:3 * W + 2 * LORA_RANK] for x in sh]
    lora_w = [_bdot(jnp.tanh(wd[z]), wup_ref[z]) for z in range(N_DIRS)]
    lora_a = [_bdot(ad[z], aup_ref[z]) for z in range(N_DIRS)]
    lora_a0b = _bdot(ad[1], aup_ref[0])
    kkr = [x * kk_ref[...] for x in k]
    kk_ss = [_bdot(x * x, ones_ref[...]) for x in kkr]

    next(inverse, None)
    next(inverse, None)

    lw = [-DECAY_SCALE * _sigmoid(w0_ref[z] + lora_w[z]) for z in range(N_DIRS)]
    a = [_sigmoid(a0_ref[z] + lora_a[z]) for z in range(N_DIRS)]
    kk = [x * lax.rsqrt(y + L2_EPS) for x, y in zip(kkr, kk_ss)]
    kt = [k[z] * (1.0 + (a[z] - 1.0) * ka_ref[...]) for z in range(N_DIRS)]
    akk = [a[z] * kk[z] for z in range(N_DIRS)]
    lw_hi = [x.astype(BF16) for x in lw]
    lw_lo = [(x - y.astype(F32)).astype(BF16) for x, y in zip(lw, lw_hi)]
    g = [jnp.dot(tri_ref[z], lw_hi[z], preferred_element_type=F32)
         + jnp.dot(tri_ref[z], lw_lo[z], preferred_element_type=F32) for z in range(N_DIRS)]
    a0_b = _sigmoid(a0_ref[0] + lora_a0b)
    kt0_b = k[1] * (1.0 + (a0_b - 1.0) * ka_ref[...])
    rk_sum = r[1] * (kt0_b + kt[1]) * rk_ref[...]
    rk_hi = rk_sum.astype(BF16)
    rk_lo = (rk_sum - rk_hi.astype(F32)).astype(BF16)
    bonus_dots = (jnp.dot(rk_hi, ones_ref[...], preferred_element_type=F32)
                  + jnp.dot(rk_lo, ones_ref[...], preferred_element_type=F32))

    quarter = n_chains // 4
    for part in range(4):
        next(inverse, None)
        apply_values(part * quarter, (part + 1) * quarter)
    next(inverse, None)
    assert len(t_inv) == n_chains
    wu = [_bdot(t, jnp.concatenate([twice(p[:L]), twice(q[:L]).astype(BF16)], axis=1))
          for t, p, q in zip(t_inv, x1, av)]
    aw = [_bdot(p, q) for p, q in zip(arb_m, wu)]
    qe = [twice(p[L:]).astype(F32) - q[:, :PAIR] for p, q in zip(x1, aw)]
    yv = [twice(p[L:]) - q[:, PAIR:] for p, q in zip(av, aw)]
    c0 = [jnp.where(blk64, _bdot_tn(p, jnp.concatenate([q, (-u[:, PAIR:]).astype(BF16)], axis=0)), 0.0)
          for p, q, u in zip(x3, v_s, wu)]
    bw = [_bdot_tn(p[n:], u[:, :PAIR]) for p, u in zip(x3, wu)]
    m_c = [jnp.where(eye, jnp.broadcast_to(g, (n, PAIR)), 0.0) - jnp.where(blk64, q, 0.0)
           for g, q in zip(gam_rows, bw)]

    lanes = [(z, p) for z in range(N_DIRS) for p in range(N_PAIRS)]
    h = [jnp.where(chains_start_sequence, 0.0, h_scr[z * N_PAIRS + p]) for z, p in lanes]
    ys = [None] * n_chains
    y_lane_lo = lax.broadcasted_iota(jnp.int32, (L, PAIR), 1) < HEAD_DIM
    for step in range(cps):
        idx = [chains.index((z, step if z == 0 else cps - 1 - step, p)) for z, p in lanes]
        y_s = [_bdot(qe[i], hh) + yv[i] for i, hh in zip(idx, h)]
        h = [_bdot(m_c[i], hh) + c0[i] for i, hh in zip(idx, h)]
        for i, y in zip(idx, y_s):
            ys[i] = jnp.where(y_lane_lo, y[:L], y[L:])
    for (z, p), hh in zip(lanes, h):
        h_scr[z * N_PAIRS + p] = hh

    def assemble(z):
        return jnp.concatenate(
            [jnp.concatenate([ys[chains.index((z, j, p))] for p in range(N_PAIRS)], axis=1)
             for j in range(cps)], axis=0)

    yf_ref[0] = assemble(0).astype(BF16)
    yb_ref[0] = assemble(1).astype(BF16)

    bon_ref[0] = (bonus_dots * v[1]).astype(BF16)
    lane_lo = lax.broadcasted_iota(jnp.int32, (L, PAIR), 1) < HEAD_DIM
    zero = jnp.zeros((L, PAIR), BF16)

    def stack(x, j, p):
        xp = x[j * L:(j + 1) * L, p * PAIR:(p + 1) * PAIR]
        return jnp.concatenate([jnp.where(lane_lo, xp, zero), jnp.where(lane_lo, zero, xp)], axis=0)

    for z in range(N_DIRS):
        g_tot = [jnp.sum(lw[z][j * L:(j + 1) * L], axis=0, keepdims=True) for j in range(cps)]
        g_tot_rows = jnp.concatenate([jnp.broadcast_to(x, (L, W)) for x in g_tot], axis=0)
        e_out = jnp.exp(-g[z])
        e_rem = jnp.exp(g_tot_rows - g[z])
        kkg = (kk[z] * jnp.exp(g[z] - lw[z])).astype(BF16)
        rg = (r[z] * jnp.exp(g[z])).astype(BF16)
        kd = (kt[z] * e_out).astype(BF16)
        bd = (akk[z] * e_out).astype(BF16)
        kdg = (kt[z] * e_rem).astype(BF16)
        bdg = (akk[z] * e_rem).astype(BF16)
        v_bf = v[z].astype(BF16)
        for j in range(cps):
            gam = jnp.broadcast_to(jnp.exp(g_tot[j]), (SUBLANES, W))
            for p in range(N_PAIRS):
                i = chains.index((z, j, p))
                rows, cols = slice(j * L, (j + 1) * L), slice(p * PAIR, (p + 1) * PAIR)
                x1_scr[i] = jnp.concatenate([kkg[rows, cols], rg[rows, cols]], axis=0)
                x2_scr[i] = jnp.concatenate([stack(kd, j, p), stack(bd, j, p)], axis=0)
                x3_scr[i] = jnp.concatenate([stack(kdg, j, p), stack(bdg, j, p)], axis=0)
                v_scr[i] = stack(v_bf, j, p)
                gam_scr[i] = gam[:, p * PAIR:(p + 1) * PAIR]


def _rwkv_call(rin, taps, tri, w_up, w0, a_up, a0, k_k, k_a, r_k, ones):
    B, T, C = rin.shape
    L = CHUNK * CHUNKS_PER_STEP
    nC = T // L
    hb = L // SUBLANES
    n_hb = T // SUBLANES
    W = RWKV_WIDTH
    n_chains = N_DIRS * CHUNKS_PER_STEP * N_PAIRS

    n_total = B * nC
    def prep(g): return jnp.minimum(g, n_total - 1)
    def chain(g): return jnp.maximum(g - 1, 0)
    def seq(blk): return blk // nC
    def pos_f(blk): return blk % nC
    def pos_b(blk): return nC - 1 - blk % nC
    def main_f(g): return (seq(prep(g)), pos_f(prep(g)), 0)
    def prev_f(g): return (seq(prep(g)), jnp.maximum(pos_f(prep(g)) * hb - 1, 0), 0)
    def next_f(g): return (seq(prep(g)), jnp.minimum((pos_f(prep(g)) + 1) * hb, n_hb - 1), 0)
    def main_b(g): return (seq(prep(g)), pos_b(prep(g)), 0)
    def prev_b(g): return (seq(prep(g)), jnp.maximum(pos_b(prep(g)) * hb - 1, 0), 0)
    def next_b(g): return (seq(prep(g)), jnp.minimum((pos_b(prep(g)) + 1) * hb, n_hb - 1), 0)
    def out_f(g): return (seq(chain(g)), pos_f(chain(g)), 0)
    def out_b(g): return (seq(chain(g)), pos_b(chain(g)), 0)

    const2 = lambda g: (0, 0)
    const3 = lambda g: (0, 0, 0)
    out_shape = tuple(jax.ShapeDtypeStruct((B, T, W), BF16) for _ in range(3))
    n = 2 * CHUNK
    return pl.pallas_call(
        functools.partial(_rwkv_kernel, n_blocks=nC),
        name="rwkv7_chunked",
        grid=(n_total + 1,),
        in_specs=[pl.BlockSpec((1, L, C), main_f),
                  pl.BlockSpec((1, SUBLANES, C), prev_f),
                  pl.BlockSpec((1, SUBLANES, C), next_f),
                  pl.BlockSpec((1, L, C), main_b),
                  pl.BlockSpec((1, SUBLANES, C), prev_b),
                  pl.BlockSpec((1, SUBLANES, C), next_b),
                  pl.BlockSpec((3, C), const2),
                  pl.BlockSpec((N_DIRS, L, L), const3),
                  pl.BlockSpec((N_DIRS, LORA_RANK, W), const3),
                  pl.BlockSpec((N_DIRS, 1, W), const3),
                  pl.BlockSpec((N_DIRS, LORA_RANK, W), const3),
                  pl.BlockSpec((N_DIRS, 1, W), const3),
                  pl.BlockSpec((1, W), const2),
                  pl.BlockSpec((1, W), const2),
                  pl.BlockSpec((1, W), const2),
                  pl.BlockSpec((W, W), const2)],
        out_specs=(pl.BlockSpec((1, L, W), out_f),
                   pl.BlockSpec((1, L, W), out_b),
                   pl.BlockSpec((1, L, W), main_b)),
        out_shape=out_shape,
        scratch_shapes=[pltpu.VMEM((N_DIRS * N_PAIRS, n, PAIR), F32),
                        pltpu.VMEM((n_chains, n, PAIR), BF16),
                        pltpu.VMEM((n_chains, 2 * n, PAIR), BF16),
                        pltpu.VMEM((n_chains, 2 * n, PAIR), BF16),
                        pltpu.VMEM((n_chains, n, PAIR), BF16),
                        pltpu.VMEM((n_chains, SUBLANES, PAIR), F32)],
        compiler_params=pltpu.CompilerParams(dimension_semantics=("arbitrary",),
                                             vmem_limit_bytes=VMEM_LIMIT),
    )(rin, rin, rin, rin, rin, rin, taps, tri, w_up, w0, a_up, a0, k_k, k_a, r_k, ones)


def _outproj_kernel(att_ref, yf_ref, yb_ref, bon_ref, grw_ref, x_ref, mod_ref, w_ref,
                    gnw_ref, gnb_ref, gpost_ref, ones_ref, o_ref):
    out_att = jnp.dot(att_ref[0], w_ref[:ATT_WIDTH, :], preferred_element_type=F32)
    y = yf_ref[0].astype(F32) + yb_ref[0].astype(F32)
    ones = ones_ref[...]
    inv_n = 1.0 / HEAD_DIM
    mu = _bdot(y, ones) * inv_n
    d = y - mu
    var = _bdot(d * d, ones) * inv_n
    yn = d * lax.rsqrt(var + GN_EPS) * gnw_ref[...] + gnb_ref[...]
    g = grw_ref[0].astype(F32)
    rw = ((yn + bon_ref[0].astype(F32)) * (g * _sigmoid(g))).astype(BF16)
    out = out_att + jnp.dot(rw, w_ref[ATT_WIDTH:, :], preferred_element_type=F32)
    ms = jnp.mean(out * out, axis=-1, keepdims=True)
    on = out * lax.rsqrt(ms + NORM_EPS) * gpost_ref[...]
    o_ref[0] = x_ref[0] + mod_ref[0] * on


def _outproj_call(att, yf, yb, bon, grw, x, mod3, w_out_bf, gn_w, gn_b, g_post, ones, tm):
    B, T, D = x.shape
    W = RWKV_WIDTH
    row = lambda t, b: (b, t, 0)
    const2 = lambda t, b: (0, 0)
    return pl.pallas_call(
        _outproj_kernel,
        name="out_proj",
        grid=(T // tm, B),
        in_specs=[pl.BlockSpec((1, tm, ATT_WIDTH), row),
                  pl.BlockSpec((1, tm, W), row),
                  pl.BlockSpec((1, tm, W), row),
                  pl.BlockSpec((1, tm, W), row),
                  pl.BlockSpec((1, tm, W), row),
                  pl.BlockSpec((1, tm, D), row),
                  pl.BlockSpec((1, 1, D), lambda t, b: (b, 0, 2)),
                  pl.BlockSpec((ATT_WIDTH + W, D), const2),
                  pl.BlockSpec((1, W), const2),
                  pl.BlockSpec((1, W), const2),
                  pl.BlockSpec((1, D), const2),
                  pl.BlockSpec((W, W), const2)],
        out_specs=pl.BlockSpec((1, tm, D), row),
        out_shape=jax.ShapeDtypeStruct((B, T, D), F32),
        compiler_params=pltpu.CompilerParams(dimension_semantics=("arbitrary", "arbitrary"),
                                             vmem_limit_bytes=VMEM_LIMIT),
    )(att, yf, yb, bon, grw, x, mod3, w_out_bf, gn_w, gn_b, g_post, ones)


def _pick_tile(T, target):
    t = min(T, target)
    while T % t:
        t //= 2
    return t


def _permute_heads(w, axis):
    blocks = jnp.split(w, ATT_Q_HEADS, axis=axis)
    return jnp.concatenate([blocks[h] for h in Q_HEAD_ORDER], axis=axis)


def kernel(x, c, w_ada, b_ada, g_pre, w_in, q_norm_g, k_norm_g, shift_taps, w_up, w0, a_up, a0,
           k_k, k_a, r_k, gn_w, gn_b, w_out, g_post):
    B, T, D = x.shape
    depth = w_ada.shape[0]
    assert T % (CHUNK * CHUNKS_PER_STEP) == 0 and T % GRID_W == 0
    tm_in = _pick_tile(T, IN_PROJ_ROWS)
    tm_out = _pick_tile(T, OUT_PROJ_ROWS)
    tq = _pick_tile(T, ATTN_QUERY_ROWS)
    cos, sin = _rope_tables(T)
    ones = _block_ones(RWKV_WIDTH)
    ti = jnp.arange(CHUNK * CHUNKS_PER_STEP)
    same_chunk = (ti[None, :] // CHUNK) == (ti[:, None] // CHUNK)
    tri = jnp.stack([same_chunk & (ti[None, :] <= ti[:, None]),
                     same_chunk & (ti[None, :] >= ti[:, None])]).astype(BF16)
    for l in range(depth):
        mod = _ada_call(c, w_ada[l], b_ada[l])
        mod3 = mod.reshape(B, 1, 3 * D)
        qg = jnp.tile(q_norm_g[l], ATT_Q_HEADS).reshape(1, ATT_WIDTH)
        kg = jnp.tile(k_norm_g[l], ATT_KV_HEADS).reshape(1, ATT_KV_WIDTH)
        wq = _permute_heads(w_in[l][:, Q_OFF:K_OFF], 1).astype(BF16)
        q, k, vt, g_att, rin, g_rw = _inproj_call(
            x, mod3, g_pre[l].reshape(1, D), w_in[l].astype(BF16), wq, qg, kg, cos, sin, ones, tm_in)
        att = _attn_call(q, k, vt, g_att, tq)
        yf, yb, bon = _rwkv_call(
            rin, shift_taps[l], tri, w_up[l].astype(BF16), w0[l].reshape(N_DIRS, 1, RWKV_WIDTH),
            a_up[l].astype(BF16), a0[l].reshape(N_DIRS, 1, RWKV_WIDTH),
            k_k[l].reshape(1, RWKV_WIDTH), k_a[l].reshape(1, RWKV_WIDTH),
            r_k[l].reshape(1, RWKV_WIDTH), ones)
        x = _outproj_call(att, yf, yb, bon, g_rw, x, mod3, w_out[l].astype(BF16),
                          gn_w[l].reshape(1, RWKV_WIDTH), gn_b[l].reshape(1, RWKV_WIDTH),
                          g_post[l].reshape(1, D), ones, tm_out)
    return x
```

```python
import functools

import jax
import jax.numpy as jnp
from jax import lax
from jax.experimental import pallas as pl
from jax.experimental.pallas import tpu as pltpu

F32 = jnp.float32
BF16 = jnp.bfloat16

HEAD_DIM = 64
ATT_Q_HEADS = 8
ATT_KV_HEADS = 2
ATT_GROUP = ATT_Q_HEADS // ATT_KV_HEADS
ATT_WIDTH = ATT_Q_HEADS * HEAD_DIM
ATT_KV_WIDTH = ATT_KV_HEADS * HEAD_DIM
RWKV_HEADS = 8
RWKV_WIDTH = RWKV_HEADS * HEAD_DIM
LORA_RANK = 64
RWKV_SHIFT_WIDTH = 3 * RWKV_WIDTH + 2 * LORA_RANK
N_DIRS = 2
GRID_W = 64
ROPE_THETA = 10000.0
DECAY_SCALE = 0.6065306597126334
NORM_EPS = 1e-6
GN_EPS = 64e-5
L2_EPS = 1e-12

Q_OFF = 0
K_OFF = Q_OFF + ATT_WIDTH
V_OFF = K_OFF + ATT_KV_WIDTH
GATT_OFF = V_OFF + ATT_KV_WIDTH
RIN_OFF = GATT_OFF + ATT_WIDTH
GRW_OFF = RIN_OFF + RWKV_SHIFT_WIDTH
IN_WIDTH = GRW_OFF + RWKV_WIDTH

LANES = 128
SUBLANES = 8
CHUNK = 64
PAIR = 2 * HEAD_DIM
BF16_SUBLANES = 16
VT_ONES = BF16_SUBLANES
VT_KV = HEAD_DIM + VT_ONES
VT_ROWS = ATT_KV_HEADS * VT_KV
Q_SCALE = HEAD_DIM ** -0.5 * 1.4426950408889634
Q_HEAD_ORDER = tuple(g + kv * ATT_GROUP for g in range(ATT_GROUP) for kv in range(ATT_KV_HEADS))
VMEM_LIMIT = 56 * 1024 * 1024
IN_PROJ_ROWS = 512
OUT_PROJ_ROWS = 512
ATTN_QUERY_ROWS = 512
ATTN_KEY_CHUNK = 256
ATTN_SCORES_AHEAD = 1
ATTN_MAX_INIT = -1e30


def _bdot(a, b):
    return jnp.dot(a.astype(BF16), b.astype(BF16), preferred_element_type=F32)


def _bdot_nt(a, b):
    return lax.dot_general(a.astype(BF16), b.astype(BF16), (((1,), (1,)), ((), ())),
                           preferred_element_type=F32)


def _bdot_tn(a, b):
    return lax.dot_general(a.astype(BF16), b.astype(BF16), (((0,), (0,)), ((), ())),
                           preferred_element_type=F32)


def _sigmoid(x):
    return 1.0 / (1.0 + jnp.exp(-x))


def _ada_kernel(c_ref, w_ref, b_ref, o_ref):
    c = c_ref[...]
    ca = c * _sigmoid(c)
    o_ref[...] = jnp.dot(ca, w_ref[...], precision=lax.Precision.HIGHEST,
                         preferred_element_type=F32) + b_ref[...]


def _ada_call(c, w_ada, b_ada):
    B, D = c.shape
    n_out = w_ada.shape[1]
    tn = D
    return pl.pallas_call(
        _ada_kernel,
        name="ada_mod",
        grid=(n_out // tn,),
        in_specs=[pl.BlockSpec((B, D), lambda j: (0, 0)),
                  pl.BlockSpec((D, tn), lambda j: (0, j)),
                  pl.BlockSpec((1, tn), lambda j: (0, j))],
        out_specs=pl.BlockSpec((B, tn), lambda j: (0, j)),
        out_shape=jax.ShapeDtypeStruct((B, n_out), F32),
        compiler_params=pltpu.CompilerParams(dimension_semantics=("arbitrary",),
                                             vmem_limit_bytes=VMEM_LIMIT),
    )(c, w_ada, b_ada.reshape(1, n_out))


def _rope(x, cos, sin_signed, first_half):
    n = x.shape[1]
    partner = jnp.where(first_half, pltpu.roll(x, n - 16, 1), pltpu.roll(x, 16, 1))
    return x * cos + partner * sin_signed


def _inproj_kernel(x_ref, mod_ref, gpre_ref, w_ref, wq_ref, qg_ref, kg_ref, cos_ref, sin_ref, ones_ref,
                   q_ref, k_ref, vt_ref, gatt_ref, rin_ref, grw_ref):
    D = x_ref.shape[2]
    x = x_ref[0]
    ms = jnp.mean(x * x, axis=-1, keepdims=True)
    xn = x * lax.rsqrt(ms + NORM_EPS) * gpre_ref[...]
    mod = mod_ref[0]
    shift = mod[:, :D]
    scale = mod[:, D:2 * D]
    h = (xn * (1.0 + scale) + shift).astype(BF16)

    def proj(lo, width):
        return jnp.dot(h, w_ref[:, lo:lo + width], preferred_element_type=F32)

    cos = cos_ref[...]
    sin = sin_ref[...]
    lane = lax.broadcasted_iota(jnp.int32, cos.shape, 1)
    first_half = (lane % 32) < 16

    def mean_sq(y):
        w = y.shape[1]
        return _bdot(y * y, ones_ref[:w, :w]) * (1.0 / HEAD_DIM)

    q = jnp.dot(h, wq_ref[...], preferred_element_type=F32)
    k = proj(K_OFF, ATT_KV_WIDTH)
    v = proj(V_OFF, ATT_KV_WIDTH).astype(BF16)
    rin_ref[0] = proj(RIN_OFF, RWKV_SHIFT_WIDTH)
    q_msq = mean_sq(q)
    k_msq = mean_sq(k)
    eye = (lax.broadcasted_iota(jnp.int32, (ATT_KV_WIDTH, ATT_KV_WIDTH), 0)
           == lax.broadcasted_iota(jnp.int32, (ATT_KV_WIDTH, ATT_KV_WIDTH), 1)).astype(BF16)
    vt = _bdot_nt(eye, v).astype(BF16)
    gatt_ref[0] = proj(GATT_OFF, ATT_WIDTH).astype(BF16)
    grw_ref[0] = proj(GRW_OFF, RWKV_WIDTH).astype(BF16)

    q = q * lax.rsqrt(q_msq + NORM_EPS) * qg_ref[...]
    reps = ATT_WIDTH // LANES
    cos_q = jnp.concatenate([cos] * reps, axis=1)
    sin_q = jnp.concatenate([sin] * reps, axis=1)
    lane_q = lax.broadcasted_iota(jnp.int32, cos_q.shape, 1)
    fh_q = (lane_q % 32) < 16
    q_ref[0] = (_rope(q, cos_q, sin_q, fh_q) * Q_SCALE).astype(BF16)
    k = k * lax.rsqrt(k_msq + NORM_EPS) * kg_ref[...]
    k_ref[0] = _rope(k, cos, sin, first_half).astype(BF16)
    ones_rows = jnp.ones((VT_ONES, vt.shape[1]), BF16)
    vt_ref[0] = jnp.concatenate([vt[:HEAD_DIM], ones_rows, vt[HEAD_DIM:], ones_rows], axis=0)


def _rope_tables(T):
    rows = T // GRID_W
    row = jnp.repeat(jnp.arange(rows, dtype=F32), GRID_W)
    col = jnp.tile(jnp.arange(GRID_W, dtype=F32), rows)
    n_freq = HEAD_DIM // 4
    inv_freq = ROPE_THETA ** (-jnp.arange(n_freq, dtype=F32) / n_freq)
    ang_r = row[:, None] * inv_freq
    ang_c = col[:, None] * inv_freq
    cr, sr, cc, sc = jnp.cos(ang_r), jnp.sin(ang_r), jnp.cos(ang_c), jnp.sin(ang_c)
    cos = jnp.concatenate([cr, cr, cc, cc], axis=1)
    sin = jnp.concatenate([-sr, sr, -sc, sc], axis=1)
    reps = LANES // HEAD_DIM
    return jnp.tile(cos, (1, reps)), jnp.tile(sin, (1, reps))


def _block_ones(n):
    i = jnp.arange(n) // HEAD_DIM
    return (i[:, None] == i[None, :]).astype(BF16)


def _inproj_call(x, mod3, g_pre, w_in_bf, wq_bf, qg, kg, cos, sin, ones, tm):
    B, T, D = x.shape
    grid = (T // tm, B)
    row = lambda t, b: (b, t, 0)
    const2 = lambda t, b: (0, 0)
    out_shapes = (
        jax.ShapeDtypeStruct((B, T, ATT_WIDTH), BF16),
        jax.ShapeDtypeStruct((B, T, ATT_KV_WIDTH), BF16),
        jax.ShapeDtypeStruct((B, VT_ROWS, T), BF16),
        jax.ShapeDtypeStruct((B, T, ATT_WIDTH), BF16),
        jax.ShapeDtypeStruct((B, T, RWKV_SHIFT_WIDTH), F32),
        jax.ShapeDtypeStruct((B, T, RWKV_WIDTH), BF16),
    )
    out_specs = tuple(
        pl.BlockSpec((1, VT_ROWS, tm), lambda t, b: (b, 0, t)) if i == 2
        else pl.BlockSpec((1, tm, s.shape[2]), row) for i, s in enumerate(out_shapes))
    return pl.pallas_call(
        _inproj_kernel,
        name="in_proj",
        grid=grid,
        in_specs=[pl.BlockSpec((1, tm, D), row),
                  pl.BlockSpec((1, 1, mod3.shape[2]), lambda t, b: (b, 0, 0)),
                  pl.BlockSpec((1, D), const2),
                  pl.BlockSpec((D, IN_WIDTH), const2),
                  pl.BlockSpec((D, ATT_WIDTH), const2),
                  pl.BlockSpec((1, ATT_WIDTH), const2),
                  pl.BlockSpec((1, ATT_KV_WIDTH), const2),
                  pl.BlockSpec((tm, LANES), lambda t, b: (t, 0)),
                  pl.BlockSpec((tm, LANES), lambda t, b: (t, 0)),
                  pl.BlockSpec((ATT_WIDTH, ATT_WIDTH), const2)],
        out_specs=out_specs,
        out_shape=out_shapes,
        compiler_params=pltpu.CompilerParams(dimension_semantics=("arbitrary", "arbitrary"),
                                             vmem_limit_bytes=VMEM_LIMIT),
    )(x, mod3, g_pre, w_in_bf, wq_bf, qg, kg, cos, sin, ones)


def _attn_kernel(q_ref, k_ref, vt_ref, g_ref, o_ref):
    q = q_ref[0]
    k = k_ref[0]
    tq = q.shape[0]
    lane_lo = lax.broadcasted_iota(jnp.int32, (tq, PAIR), 1) < HEAD_DIM
    zero = jnp.zeros((tq, PAIR), BF16)
    T = k.shape[0]
    kc = min(T, ATTN_KEY_CHUNK)
    n_kc = T // kc
    qs = []
    for j in range(ATT_GROUP):
        qp = q[:, j * PAIR:(j + 1) * PAIR]
        qs.append(jnp.concatenate([jnp.where(lane_lo, qp, zero), jnp.where(lane_lo, zero, qp)], axis=0))

    def scores(item):
        j, c = item
        return lax.dot_general(k[c * kc:(c + 1) * kc], qs[j], (((1,), (1,)), ((), ())),
                               preferred_element_type=F32)

    items = [(j, c) for j in range(ATT_GROUP) for c in range(n_kc)]
    outs = [None] * ATT_Q_HEADS
    pending = [scores(item) for item in items[:ATTN_SCORES_AHEAD]]
    for idx, (j, c) in enumerate(items):
        s = pending.pop(0)
        if idx + ATTN_SCORES_AHEAD < len(items):
            pending.append(scores(items[idx + ATTN_SCORES_AHEAD]))
        if c == 0:
            m = jnp.full((1, 2 * tq), ATTN_MAX_INIT, F32)
            acc = [jnp.zeros((VT_KV, tq), F32) for _ in range(ATT_KV_HEADS)]
        m_new = jnp.maximum(m, jnp.max(s, axis=0, keepdims=True))
        alpha = jnp.exp2(m - m_new)
        p = jnp.exp2(s - m_new).astype(BF16)
        m = m_new
        for kv in range(ATT_KV_HEADS):
            vt = vt_ref[0, kv * VT_KV:(kv + 1) * VT_KV, c * kc:(c + 1) * kc]
            acc[kv] = (acc[kv] * alpha[:, kv * tq:(kv + 1) * tq]
                       + jnp.dot(vt, p[:, kv * tq:(kv + 1) * tq], preferred_element_type=F32))
        if c == n_kc - 1:
            for kv in range(ATT_KV_HEADS):
                o = acc[kv]
                outs[Q_HEAD_ORDER[j * ATT_KV_HEADS + kv]] = o[:HEAD_DIM] / o[HEAD_DIM:HEAD_DIM + 1]
    y = jnp.concatenate(outs, axis=0).T
    g = g_ref[0].astype(F32)
    o_ref[0] = (y * (g * _sigmoid(g))).astype(BF16)


def _attn_call(q, k, vt, g_att, tq):
    B, T, _ = q.shape
    return pl.pallas_call(
        _attn_kernel,
        name="gqa_attn",
        grid=(B, T // tq),
        in_specs=[pl.BlockSpec((1, tq, ATT_WIDTH), lambda b, i: (b, i, 0)),
                  pl.BlockSpec((1, T, ATT_KV_WIDTH), lambda b, i: (b, 0, 0)),
                  pl.BlockSpec((1, VT_ROWS, T), lambda b, i: (b, 0, 0)),
                  pl.BlockSpec((1, tq, ATT_WIDTH), lambda b, i: (b, i, 0))],
        out_specs=pl.BlockSpec((1, tq, ATT_WIDTH), lambda b, i: (b, i, 0)),
        out_shape=jax.ShapeDtypeStruct((B, T, ATT_WIDTH), BF16),
        compiler_params=pltpu.CompilerParams(dimension_semantics=("arbitrary", "arbitrary"),
                                             vmem_limit_bytes=VMEM_LIMIT),
    )(q, k, vt, g_att)


CHUNKS_PER_STEP = 4
N_PAIRS = RWKV_WIDTH // PAIR


def _later_rows(x, s, odd):
    first = s if odd else 0
    return jnp.concatenate([x[b:b + s] for b in range(first, x.shape[0], 2 * s)], axis=0)


def _merge_rows(keep, new, s, odd):
    out = []
    for i, b in enumerate(range(0, keep.shape[0], 2 * s)):
        lo, hi = keep[b:b + s], keep[b + s:b + 2 * s]
        blk = new[i * s:(i + 1) * s]
        out += [lo, blk] if odd else [blk, hi]
    return jnp.concatenate(out, axis=0)


INVERSE_STAGES = 9


def _inverse_stages(a_list, eye, blks, odds, out):
    blk8, blk16, blk32, blk64 = blks
    a8 = [jnp.where(blk8, a, 0.0) for a in a_list]
    a2 = [_bdot(x, x) for x in a8]
    yield
    a4 = [_bdot(x, x) for x in a2]
    t = [_bdot(eye - x, eye + y) for x, y in zip(a8, a2)]
    yield
    t = [_bdot(x, eye + y) for x, y in zip(t, a4)]
    for inner, outer, s in ((blk8, blk16, 8), (blk16, blk32, 16), (blk32, blk64, 32)):
        yield
        off = jnp.logical_and(outer, jnp.logical_not(inner))
        xs = [_bdot(_later_rows(x, s, o), jnp.where(off, a, 0.0)) for x, a, o in zip(t, a_list, odds)]
        yield
        upd = [_later_rows(x, s, o) - _bdot(y, x) for x, y, o in zip(t, xs, odds)]
        t = [_merge_rows(x, u, s, o) for x, u, o in zip(t, upd, odds)]
    out.extend(t)


def _rwkv_kernel(fm_ref, fp_ref, fn_ref, bm_ref, bp_ref, bn_ref, taps_ref, tri_ref,
                 wup_ref, w0_ref, aup_ref, a0_ref, kk_ref, ka_ref, rk_ref, ones_ref,
                 yf_ref, yb_ref, bon_ref,
                 h_scr, x1_scr, x2_scr, x3_scr, v_scr, gam_scr, *, n_blocks):
    g_step = pl.program_id(0)
    last_block = pl.num_programs(0) - 2
    s = lax.rem(jnp.minimum(g_step, last_block), jnp.int32(n_blocks))
    chains_start_sequence = lax.rem(jnp.maximum(g_step - 1, 0), jnp.int32(n_blocks)) == 0

    @pl.when(g_step == 0)
    def _():
        def zero_chain(i, carry):
            for ref in (x1_scr, x2_scr, x3_scr, v_scr, gam_scr):
                ref[i] = jnp.zeros(ref.shape[1:], ref.dtype)
            return carry

        def zero_state(i, carry):
            h_scr[i] = jnp.zeros(h_scr.shape[1:], h_scr.dtype)
            return carry

        lax.fori_loop(0, x1_scr.shape[0], zero_chain, 0)
        lax.fori_loop(0, h_scr.shape[0], zero_state, 0)

    SL = fm_ref.shape[1]
    L = CHUNK
    cps = SL // L
    W = RWKV_WIDTH
    n = 2 * L
    n_chains = x1_scr.shape[0]

    ri = lax.broadcasted_iota(jnp.int32, (n, n), 0)
    ci = lax.broadcasted_iota(jnp.int32, (n, n), 1)
    eye = ri == ci
    blk = lambda size: (ri // size) == (ci // size)
    blk8, blk16, blk32, blk64 = blk(8), blk(16), blk(32), blk(64)
    stricts = (jnp.logical_and(blk64, ci < ri), jnp.logical_and(blk64, ci > ri))
    chains = [(z, j, p) for z in range(N_DIRS) for j in range(cps) for p in range(N_PAIRS)]
    assert len(chains) == n_chains
    strict = [stricts[z] for z, _, _ in chains]
    incl = [jnp.logical_or(m, eye) for m in strict]
    odds = [z == 0 for z, _, _ in chains]
    blks = (blk8, blk16, blk32, blk64)

    x1 = [x1_scr[i] for i in range(n_chains)]
    twice = lambda x: jnp.concatenate([x, x], axis=0)
    x2 = [x2_scr[i] for i in range(n_chains)]
    x3 = [x3_scr[i] for i in range(n_chains)]
    v_s = [v_scr[i] for i in range(n_chains)]
    gam_rows = [gam_scr[i][0:1] for i in range(n_chains)]
    gram = [_bdot_nt(a, b) for a, b in zip(x1, x2)]
    akb_m = [jnp.where(m, twice(g[:L, n:]), 0.0) for m, g in zip(strict, gram)]
    arb_m = [jnp.where(m, twice(g[L:, n:]), 0.0) for m, g in zip(incl, gram)]
    rt = lax.broadcasted_iota(jnp.int32, (L, n), 0)
    ct = lax.broadcasted_iota(jnp.int32, (L, n), 1) % L
    strict_t = (ct < rt, ct > rt)
    incl_t = tuple(jnp.logical_or(m, ct == rt) for m in strict_t)

    t_inv = []
    inverse = _inverse_stages(akb_m, eye, blks, odds, t_inv)
    av = [None] * n_chains

    def apply_values(lo, hi):
        for i in range(lo, hi):
            z = chains[i][0]
            a_kd = jnp.concatenate([jnp.where(strict_t[z], gram[i][:L, :n], 0.0),
                                    jnp.where(incl_t[z], gram[i][L:, :n], 0.0)], axis=0)
            av[i] = _bdot(a_kd, v_s[i])

    next(inverse, None)
    next(inverse, None)

    taps = taps_ref[...]
    row8 = lax.broadcasted_iota(jnp.int32, (SUBLANES, 1), 0)

    def shifted(m_ref, p_ref, n_ref, block):
        main = m_ref[0]
        prev_row = jnp.where(block > 0, p_ref[0][SUBLANES - 1:SUBLANES], 0.0)
        next_row = jnp.where(block < n_blocks - 1, n_ref[0][0:1], 0.0)
        up = pltpu.roll(main, 1, 0)
        dn = pltpu.roll(main, SL - 1, 0)
        up = jnp.concatenate([jnp.where(row8 == 0, prev_row, up[:SUBLANES]), up[SUBLANES:]], axis=0)
        dn = jnp.concatenate([dn[:SL - SUBLANES],
                              jnp.where(row8 == SUBLANES - 1, next_row, dn[SL - SUBLANES:])], axis=0)
        return taps[0:1] * up + taps[1:2] * main + taps[2:3] * dn

    cf = s
    cb = n_blocks - 1 - s
    sh = (shifted(fm_ref, fp_ref, fn_ref, cf), shifted(bm_ref, bp_ref, bn_ref, cb))
    r = [x[:, 0:W] for x in sh]
    k = [x[:, W:2 * W] for x in sh]
    v = [x[:, 2 * W:3 * W] for x in sh]
    wd = [x[:, 3 * W:3 * W + LORA_RANK] for x in sh]
    ad = [x[:, 3 * W + LORA_RANK:3 * W + 2 * LORA_RANK] for x in sh]
    lora_w = [_bdot(jnp.tanh(wd[z]), wup_ref[z]) for z in range(N_DIRS)]
    lora_a = [_bdot(ad[z], aup_ref[z]) for z in range(N_DIRS)]
    lora_a0b = _bdot(ad[1], aup_ref[0])
    kkr = [x * kk_ref[...] for x in k]
    kk_ss = [_bdot(x * x, ones_ref[...]) for x in kkr]

    next(inverse, None)
    next(inverse, None)

    lw = [-DECAY_SCALE * _sigmoid(w0_ref[z] + lora_w[z]) for z in range(N_DIRS)]
    a = [_sigmoid(a0_ref[z] + lora_a[z]) for z in range(N_DIRS)]
    kk = [x * lax.rsqrt(y + L2_EPS) for x, y in zip(kkr, kk_ss)]
    kt = [k[z] * (1.0 + (a[z] - 1.0) * ka_ref[...]) for z in range(N_DIRS)]
    akk = [a[z] * kk[z] for z in range(N_DIRS)]
    lw_hi = [x.astype(BF16) for x in lw]
    lw_lo = [(x - y.astype(F32)).astype(BF16) for x, y in zip(lw, lw_hi)]
    g = [jnp.dot(tri_ref[z], lw_hi[z], preferred_element_type=F32)
         + jnp.dot(tri_ref[z], lw_lo[z], preferred_element_type=F32) for z in range(N_DIRS)]
    a0_b = _sigmoid(a0_ref[0] + lora_a0b)
    kt0_b = k[1] * (1.0 + (a0_b - 1.0) * ka_ref[...])
    rk_sum = r[1] * (kt0_b + kt[1]) * rk_ref[...]
    rk_hi = rk_sum.astype(BF16)
    rk_lo = (rk_sum - rk_hi.astype(F32)).astype(BF16)
    bonus_dots = (jnp.dot(rk_hi, ones_ref[...], preferred_element_type=F32)
                  + jnp.dot(rk_lo, ones_ref[...], preferred_element_type=F32))

    quarter = n_chains // 4
    for part in range(4):
        next(inverse, None)
        apply_values(part * quarter, (part + 1) * quarter)
    next(inverse, None)
    assert len(t_inv) == n_chains
    wu = [_bdot(t, jnp.concatenate([twice(p[:L]), twice(q[:L]).astype(BF16)], axis=1))
          for t, p, q in zip(t_inv, x1, av)]
    aw = [_bdot(p, q) for p, q in zip(arb_m, wu)]
    qe = [twice(p[L:]).astype(F32) - q[:, :PAIR] for p, q in zip(x1, aw)]
    yv = [twice(p[L:]) - q[:, PAIR:] for p, q in zip(av, aw)]
    c0 = [jnp.where(blk64, _bdot_tn(p, jnp.concatenate([q, (-u[:, PAIR:]).astype(BF16)], axis=0)), 0.0)
          for p, q, u in zip(x3, v_s, wu)]
    bw = [_bdot_tn(p[n:], u[:, :PAIR]) for p, u in zip(x3, wu)]
    m_c = [jnp.where(eye, jnp.broadcast_to(g, (n, PAIR)), 0.0) - jnp.where(blk64, q, 0.0)
           for g, q in zip(gam_rows, bw)]

    lanes = [(z, p) for z in range(N_DIRS) for p in range(N_PAIRS)]
    h = [jnp.where(chains_start_sequence, 0.0, h_scr[z * N_PAIRS + p]) for z, p in lanes]
    ys = [None] * n_chains
    y_lane_lo = lax.broadcasted_iota(jnp.int32, (L, PAIR), 1) < HEAD_DIM
    for step in range(cps):
        idx = [chains.index((z, step if z == 0 else cps - 1 - step, p)) for z, p in lanes]
        y_s = [_bdot(qe[i], hh) + yv[i] for i, hh in zip(idx, h)]
        h = [_bdot(m_c[i], hh) + c0[i] for i, hh in zip(idx, h)]
        for i, y in zip(idx, y_s):
            ys[i] = jnp.where(y_lane_lo, y[:L], y[L:])
    for (z, p), hh in zip(lanes, h):
        h_scr[z * N_PAIRS + p] = hh

    def assemble(z):
        return jnp.concatenate(
            [jnp.concatenate([ys[chains.index((z, j, p))] for p in range(N_PAIRS)], axis=1)
             for j in range(cps)], axis=0)

    yf_ref[0] = assemble(0).astype(BF16)
    yb_ref[0] = assemble(1).astype(BF16)

    bon_ref[0] = (bonus_dots * v[1]).astype(BF16)
    lane_lo = lax.broadcasted_iota(jnp.int32, (L, PAIR), 1) < HEAD_DIM
    zero = jnp.zeros((L, PAIR), BF16)

    def stack(x, j, p):
        xp = x[j * L:(j + 1) * L, p * PAIR:(p + 1) * PAIR]
        return jnp.concatenate([jnp.where(lane_lo, xp, zero), jnp.where(lane_lo, zero, xp)], axis=0)

    for z in range(N_DIRS):
        g_tot = [jnp.sum(lw[z][j * L:(j + 1) * L], axis=0, keepdims=True) for j in range(cps)]
        g_tot_rows = jnp.concatenate([jnp.broadcast_to(x, (L, W)) for x in g_tot], axis=0)
        e_out = jnp.exp(-g[z])
        e_rem = jnp.exp(g_tot_rows - g[z])
        kkg = (kk[z] * jnp.exp(g[z] - lw[z])).astype(BF16)
        rg = (r[z] * jnp.exp(g[z])).astype(BF16)
        kd = (kt[z] * e_out).astype(BF16)
        bd = (akk[z] * e_out).astype(BF16)
        kdg = (kt[z] * e_rem).astype(BF16)
        bdg = (akk[z] * e_rem).astype(BF16)
        v_bf = v[z].astype(BF16)
        for j in range(cps):
            gam = jnp.broadcast_to(jnp.exp(g_tot[j]), (SUBLANES, W))
            for p in range(N_PAIRS):
                i = chains.index((z, j, p))
                rows, cols = slice(j * L, (j + 1) * L), slice(p * PAIR, (p + 1) * PAIR)
                x1_scr[i] = jnp.concatenate([kkg[rows, cols], rg[rows, cols]], axis=0)
                x2_scr[i] = jnp.concatenate([stack(kd, j, p), stack(bd, j, p)], axis=0)
                x3_scr[i] = jnp.concatenate([stack(kdg, j, p), stack(bdg, j, p)], axis=0)
                v_scr[i] = stack(v_bf, j, p)
                gam_scr[i] = gam[:, p * PAIR:(p + 1) * PAIR]


def _rwkv_call(rin, taps, tri, w_up, w0, a_up, a0, k_k, k_a, r_k, ones):
    B, T, C = rin.shape
    L = CHUNK * CHUNKS_PER_STEP
    nC = T // L
    hb = L // SUBLANES
    n_hb = T // SUBLANES
    W = RWKV_WIDTH
    n_chains = N_DIRS * CHUNKS_PER_STEP * N_PAIRS

    n_total = B * nC
    def prep(g): return jnp.minimum(g, n_total - 1)
    def chain(g): return jnp.maximum(g - 1, 0)
    def seq(blk): return blk // nC
    def pos_f(blk): return blk % nC
    def pos_b(blk): return nC - 1 - blk % nC
    def main_f(g): return (seq(prep(g)), pos_f(prep(g)), 0)
    def prev_f(g): return (seq(prep(g)), jnp.maximum(pos_f(prep(g)) * hb - 1, 0), 0)
    def next_f(g): return (seq(prep(g)), jnp.minimum((pos_f(prep(g)) + 1) * hb, n_hb - 1), 0)
    def main_b(g): return (seq(prep(g)), pos_b(prep(g)), 0)
    def prev_b(g): return (seq(prep(g)), jnp.maximum(pos_b(prep(g)) * hb - 1, 0), 0)
    def next_b(g): return (seq(prep(g)), jnp.minimum((pos_b(prep(g)) + 1) * hb, n_hb - 1), 0)
    def out_f(g): return (seq(chain(g)), pos_f(chain(g)), 0)
    def out_b(g): return (seq(chain(g)), pos_b(chain(g)), 0)

    const2 = lambda g: (0, 0)
    const3 = lambda g: (0, 0, 0)
    out_shape = tuple(jax.ShapeDtypeStruct((B, T, W), BF16) for _ in range(3))
    n = 2 * CHUNK
    return pl.pallas_call(
        functools.partial(_rwkv_kernel, n_blocks=nC),
        name="rwkv7_chunked",
        grid=(n_total + 1,),
        in_specs=[pl.BlockSpec((1, L, C), main_f),
                  pl.BlockSpec((1, SUBLANES, C), prev_f),
                  pl.BlockSpec((1, SUBLANES, C), next_f),
                  pl.BlockSpec((1, L, C), main_b),
                  pl.BlockSpec((1, SUBLANES, C), prev_b),
                  pl.BlockSpec((1, SUBLANES, C), next_b),
                  pl.BlockSpec((3, C), const2),
                  pl.BlockSpec((N_DIRS, L, L), const3),
                  pl.BlockSpec((N_DIRS, LORA_RANK, W), const3),
                  pl.BlockSpec((N_DIRS, 1, W), const3),
                  pl.BlockSpec((N_DIRS, LORA_RANK, W), const3),
                  pl.BlockSpec((N_DIRS, 1, W), const3),
                  pl.BlockSpec((1, W), const2),
                  pl.BlockSpec((1, W), const2),
                  pl.BlockSpec((1, W), const2),
                  pl.BlockSpec((W, W), const2)],
        out_specs=(pl.BlockSpec((1, L, W), out_f),
                   pl.BlockSpec((1, L, W), out_b),
                   pl.BlockSpec((1, L, W), main_b)),
        out_shape=out_shape,
        scratch_shapes=[pltpu.VMEM((N_DIRS * N_PAIRS, n, PAIR), F32),
                        pltpu.VMEM((n_chains, n, PAIR), BF16),
                        pltpu.VMEM((n_chains, 2 * n, PAIR), BF16),
                        pltpu.VMEM((n_chains, 2 * n, PAIR), BF16),
                        pltpu.VMEM((n_chains, n, PAIR), BF16),
                        pltpu.VMEM((n_chains, SUBLANES, PAIR), F32)],
        compiler_params=pltpu.CompilerParams(dimension_semantics=("arbitrary",),
                                             vmem_limit_bytes=VMEM_LIMIT),
    )(rin, rin, rin, rin, rin, rin, taps, tri, w_up, w0, a_up, a0, k_k, k_a, r_k, ones)


def _outproj_kernel(att_ref, yf_ref, yb_ref, bon_ref, grw_ref, x_ref, mod_ref, w_ref,
                    gnw_ref, gnb_ref, gpost_ref, ones_ref, o_ref):
    out_att = jnp.dot(att_ref[0], w_ref[:ATT_WIDTH, :], preferred_element_type=F32)
    y = yf_ref[0].astype(F32) + yb_ref[0].astype(F32)
    ones = ones_ref[...]
    inv_n = 1.0 / HEAD_DIM
    mu = _bdot(y, ones) * inv_n
    d = y - mu
    var = _bdot(d * d, ones) * inv_n
    yn = d * lax.rsqrt(var + GN_EPS) * gnw_ref[...] + gnb_ref[...]
    g = grw_ref[0].astype(F32)
    rw = ((yn + bon_ref[0].astype(F32)) * (g * _sigmoid(g))).astype(BF16)
    out = out_att + jnp.dot(rw, w_ref[ATT_WIDTH:, :], preferred_element_type=F32)
    ms = jnp.mean(out * out, axis=-1, keepdims=True)
    on = out * lax.rsqrt(ms + NORM_EPS) * gpost_ref[...]
    o_ref[0] = x_ref[0] + mod_ref[0] * on


def _outproj_call(att, yf, yb, bon, grw, x, mod3, w_out_bf, gn_w, gn_b, g_post, ones, tm):
    B, T, D = x.shape
    W = RWKV_WIDTH
    row = lambda t, b: (b, t, 0)
    const2 = lambda t, b: (0, 0)
    return pl.pallas_call(
        _outproj_kernel,
        name="out_proj",
        grid=(T // tm, B),
        in_specs=[pl.BlockSpec((1, tm, ATT_WIDTH), row),
                  pl.BlockSpec((1, tm, W), row),
                  pl.BlockSpec((1, tm, W), row),
                  pl.BlockSpec((1, tm, W), row),
                  pl.BlockSpec((1, tm, W), row),
                  pl.BlockSpec((1, tm, D), row),
                  pl.BlockSpec((1, 1, D), lambda t, b: (b, 0, 2)),
                  pl.BlockSpec((ATT_WIDTH + W, D), const2),
                  pl.BlockSpec((1, W), const2),
                  pl.BlockSpec((1, W), const2),
                  pl.BlockSpec((1, D), const2),
                  pl.BlockSpec((W, W), const2)],
        out_specs=pl.BlockSpec((1, tm, D), row),
        out_shape=jax.ShapeDtypeStruct((B, T, D), F32),
        compiler_params=pltpu.CompilerParams(dimension_semantics=("arbitrary", "arbitrary"),
                                             vmem_limit_bytes=VMEM_LIMIT),
    )(att, yf, yb, bon, grw, x, mod3, w_out_bf, gn_w, gn_b, g_post, ones)


def _pick_tile(T, target):
    t = min(T, target)
    while T % t:
        t //= 2
    return t


def _permute_heads(w, axis):
    blocks = jnp.split(w, ATT_Q_HEADS, axis=axis)
    return jnp.concatenate([blocks[h] for h in Q_HEAD_ORDER], axis=axis)


def kernel(x, c, w_ada, b_ada, g_pre, w_in, q_norm_g, k_norm_g, shift_taps, w_up, w0, a_up, a0,
           k_k, k_a, r_k, gn_w, gn_b, w_out, g_post):
    B, T, D = x.shape
    depth = w_ada.shape[0]
    assert T % (CHUNK * CHUNKS_PER_STEP) == 0 and T % GRID_W == 0
    tm_in = _pick_tile(T, IN_PROJ_ROWS)
    tm_out = _pick_tile(T, OUT_PROJ_ROWS)
    tq = _pick_tile(T, ATTN_QUERY_ROWS)
    cos, sin = _rope_tables(T)
    ones = _block_ones(RWKV_WIDTH)
    ti = jnp.arange(CHUNK * CHUNKS_PER_STEP)
    same_chunk = (ti[None, :] // CHUNK) == (ti[:, None] // CHUNK)
    tri = jnp.stack([same_chunk & (ti[None, :] <= ti[:, None]),
                     same_chunk & (ti[None, :] >= ti[:, None])]).astype(BF16)
    for l in range(depth):
        mod = _ada_call(c, w_ada[l], b_ada[l])
        mod3 = mod.reshape(B, 1, 3 * D)
        qg = jnp.tile(q_norm_g[l], ATT_Q_HEADS).reshape(1, ATT_WIDTH)
        kg = jnp.tile(k_norm_g[l], ATT_KV_HEADS).reshape(1, ATT_KV_WIDTH)
        wq = _permute_heads(w_in[l][:, Q_OFF:K_OFF], 1).astype(BF16)
        q, k, vt, g_att, rin, g_rw = _inproj_call(
            x, mod3, g_pre[l].reshape(1, D), w_in[l].astype(BF16), wq, qg, kg, cos, sin, ones, tm_in)
        att = _attn_call(q, k, vt, g_att, tq)
        yf, yb, bon = _rwkv_call(
            rin, shift_taps[l], tri, w_up[l].astype(BF16), w0[l].reshape(N_DIRS, 1, RWKV_WIDTH),
            a_up[l].astype(BF16), a0[l].reshape(N_DIRS, 1, RWKV_WIDTH),
            k_k[l].reshape(1, RWKV_WIDTH), k_a[l].reshape(1, RWKV_WIDTH),
            r_k[l].reshape(1, RWKV_WIDTH), ones)
        x = _outproj_call(att, yf, yb, bon, g_rw, x, mod3, w_out[l].astype(BF16),
                          gn_w[l].reshape(1, RWKV_WIDTH), gn_b[l].reshape(1, RWKV_WIDTH),
                          g_post[l].reshape(1, D), ones, tm_out)
    return x
```
